```python
import math
import jax, jax.numpy as jnp
from jax import lax
import numpy as np

D_MODEL = 2048
BATCH = 4
SEQ = 8192
DEPTH = 1

MIX_WIDTH = D_MODEL
GMLP_WIDTH = MIX_WIDTH // 2
GMLP_GROUPS = 8
GMLP_DG = GMLP_WIDTH // GMLP_GROUPS
CHUNK = 128
DIFF_WIDTH = MIX_WIDTH - GMLP_WIDTH
DIFF_HEADS = 8
DIFF_DV = DIFF_WIDTH // DIFF_HEADS
DIFF_DK = DIFF_DV // 2
QK_WIDTH = DIFF_HEADS * 2 * DIFF_DK
IN_WIDTH = 2 * GMLP_WIDTH + 2 * QK_WIDTH + DIFF_WIDTH
Q_BLOCK = 128
D_FF = 4 * D_MODEL
EPS = 1e-6

kernel_name = "hybrid_gmlp_diffattn_alibi_block"


def rms_norm(x, g):
    xf = x.astype(jnp.float32)
    y = xf * lax.rsqrt(jnp.mean(xf * xf, axis=-1, keepdims=True) + EPS)
    return (y * g.astype(jnp.float32)).astype(x.dtype)


def layer_norm(x, g, b):
    xf = x.astype(jnp.float32)
    mu = jnp.mean(xf, axis=-1, keepdims=True)
    xc = xf - mu
    var = jnp.mean(xc * xc, axis=-1, keepdims=True)
    y = xc * lax.rsqrt(var + EPS) * g.astype(jnp.float32) + b.astype(jnp.float32)
    return y.astype(x.dtype)


def alibi_slopes(n):
    return jnp.asarray(2.0 ** (-8.0 * np.arange(1, n + 1) / n), dtype=jnp.float32)


def chunked_spatial_gating(z, ln_g, ln_b, w_s, b_s):
    B, S, _ = z.shape
    z = z.reshape(B, S, GMLP_GROUPS, 2, GMLP_DG)
    u, v = z[..., 0, :], z[..., 1, :]
    v = layer_norm(v, ln_g, ln_b)
    causal = jnp.tril(jnp.ones((CHUNK, CHUNK), dtype=bool))
    w = jnp.where(causal, w_s, 0).astype(v.dtype)
    vc = v.reshape(B, S // CHUNK, CHUNK, GMLP_GROUPS, GMLP_DG)
    mixed = jnp.einsum('gts,bcsgd->bctgd', w, vc) + b_s.T[:, :, None].astype(v.dtype)
    return u * mixed.reshape(B, S, GMLP_GROUPS, GMLP_DG)


def diff_attention(q, k, v, lam, slopes):
    B, S = q.shape[0], q.shape[1]
    nb = S // Q_BLOCK
    scale = DIFF_DK ** -0.5
    qb = jnp.moveaxis(q.reshape(B, nb, Q_BLOCK, DIFF_HEADS, 2, DIFF_DK), 1, 0)
    kpos = jnp.arange(S, dtype=jnp.int32)

    def block(args):
        q_blk, start = args
        qpos = start + jnp.arange(Q_BLOCK, dtype=jnp.int32)
        dist = qpos[:, None] - kpos[None, :]
        s = jnp.einsum('bqhmd,bkhmd->bhmqk', q_blk, k).astype(jnp.float32) * scale
        s = s - slopes[None, :, None, None, None] * dist.astype(jnp.float32)
        s = jnp.where(dist >= 0, s, -jnp.inf)
        p = jax.nn.softmax(s, axis=-1)
        a = p[:, :, 0] - lam * p[:, :, 1]
        return jnp.einsum('bhqk,bkhd->bqhd', a.astype(v.dtype), v)

    o = lax.map(block, (qb, jnp.arange(nb, dtype=jnp.int32) * Q_BLOCK))
    return jnp.moveaxis(o, 0, 1).reshape(B, S, DIFF_HEADS, DIFF_DV)


def setup_inputs(seed: int = 0) -> dict:
    key = jax.random.key(seed)
    ks = jax.random.split(key, 20)
    f32 = jnp.float32
    n = lambda k, shape: jax.random.normal(k, shape, dtype=f32)
    L = DEPTH
    return {
        "x": n(ks[0], (BATCH, SEQ, D_MODEL)),
        "pre_mix_g": 1.0 + 0.05 * n(ks[1], (L, D_MODEL)),
        "w_in": n(ks[2], (L, D_MODEL, IN_WIDTH)) * D_MODEL ** -0.5,
        "gmlp_ln_g": 1.0 + 0.05 * n(ks[3], (L, GMLP_GROUPS, GMLP_DG)),
        "gmlp_ln_b": 0.02 * n(ks[4], (L, GMLP_GROUPS, GMLP_DG)),
        "gmlp_w_s": n(ks[5], (L, GMLP_GROUPS, CHUNK, CHUNK)) * CHUNK ** -0.5,
        "gmlp_b_s": 1.0 + 0.1 * n(ks[6], (L, GMLP_GROUPS, CHUNK)),
        "lambda_q1": 0.1 * n(ks[7], (L, DIFF_DK)),
        "lambda_k1": 0.1 * n(ks[8], (L, DIFF_DK)),
        "lambda_q2": 0.1 * n(ks[9], (L, DIFF_DK)),
        "lambda_k2": 0.1 * n(ks[10], (L, DIFF_DK)),
        "diff_subln_g": 1.0 + 0.05 * n(ks[11], (L, DIFF_DV)),
        "w_out": n(ks[12], (L, MIX_WIDTH, D_MODEL)) * MIX_WIDTH ** -0.5,
        "post_mix_g": 1.0 + 0.05 * n(ks[13], (L, D_MODEL)),
        "pre_mlp_g": 1.0 + 0.05 * n(ks[14], (L, D_MODEL)),
        "w_up": n(ks[15], (L, D_MODEL, D_FF)) * D_MODEL ** -0.5,
        "w_down": n(ks[16], (L, D_FF, D_MODEL)) * D_FF ** -0.5,
        "post_mlp_g": 1.0 + 0.05 * n(ks[17], (L, D_MODEL)),
    }


def reference(x, pre_mix_g, w_in, gmlp_ln_g, gmlp_ln_b, gmlp_w_s, gmlp_b_s,
              lambda_q1, lambda_k1, lambda_q2, lambda_k2, diff_subln_g, w_out,
              post_mix_g, pre_mlp_g, w_up, w_down, post_mlp_g):
    B, S, _ = x.shape
    slopes = alibi_slopes(DIFF_HEADS)
    h = x
    for l in range(DEPTH):
        lambda_init = 0.8 - 0.6 * math.exp(-0.3 * l)
        xn = rms_norm(h, pre_mix_g[l])
        proj = xn @ w_in[l]
        z_g, q, k, v = jnp.split(
            proj, [2 * GMLP_WIDTH, 2 * GMLP_WIDTH + QK_WIDTH, 2 * GMLP_WIDTH + 2 * QK_WIDTH], axis=-1)
        y_a = chunked_spatial_gating(jax.nn.gelu(z_g), gmlp_ln_g[l], gmlp_ln_b[l],
                                     gmlp_w_s[l], gmlp_b_s[l]).reshape(B, S, GMLP_WIDTH)
        lam = (jnp.exp(jnp.sum(lambda_q1[l].astype(jnp.float32) * lambda_k1[l].astype(jnp.float32)))
               - jnp.exp(jnp.sum(lambda_q2[l].astype(jnp.float32) * lambda_k2[l].astype(jnp.float32)))
               + lambda_init)
        o = diff_attention(q.reshape(B, S, DIFF_HEADS, 2, DIFF_DK),
                           k.reshape(B, S, DIFF_HEADS, 2, DIFF_DK),
                           v.reshape(B, S, DIFF_HEADS, DIFF_DV), lam, slopes)
        y_b = (rms_norm(o, diff_subln_g[l]) * (1.0 - lambda_init)).reshape(B, S, DIFF_WIDTH)
        y = jnp.concatenate([y_a, y_b], axis=-1) @ w_out[l]
        h = h + rms_norm(y, post_mix_g[l])
        hn = rms_norm(h, pre_mlp_g[l])
        f = jnp.square(jax.nn.relu(hn @ w_up[l])) @ w_down[l]
        h = h + rms_norm(f, post_mlp_g[l])
    return h
```

```python
import functools
import math

import jax
import jax.numpy as jnp
import numpy as np
from jax import lax
from jax.experimental import pallas as pl
from jax.experimental.pallas import tpu as pltpu

EPS = 1e-6
GROUPS = 8
GROUP_CH = 128
CHUNK = 128
HEADS = 8
HEAD_DV = 128
HEAD_DK = 64
LANES = 128
VMEM_LIMIT_BYTES = 56 * 1024 * 1024

F32 = jnp.float32
BF16 = jnp.bfloat16


def _rms(x, g):
    return x * lax.rsqrt(jnp.mean(x * x, axis=-1, keepdims=True) + EPS) * g


def _gelu_tanh(x):
    c = math.sqrt(2.0 / math.pi)
    return 0.5 * x * (1.0 + jnp.tanh(c * (x + 0.044715 * (x * x * x))))


def _in_proj_kernel(x_ref, g_ref, w_ref, o_ref, xn_ref, *, gelu_tiles):
    j = pl.program_id(1)

    @pl.when(j == 0)
    def _():
        xn_ref[...] = _rms(x_ref[...], g_ref[...]).astype(BF16)

    acc = jnp.dot(xn_ref[...], w_ref[...], preferred_element_type=F32)

    @pl.when(j < gelu_tiles)
    def _():
        o_ref[...] = _gelu_tanh(acc).astype(BF16)

    @pl.when(j >= gelu_tiles)
    def _():
        o_ref[...] = acc.astype(BF16)


def _in_proj(x2, g, w, *, gelu_cols, tm, tn):
    n, d = x2.shape
    width = w.shape[1]
    return pl.pallas_call(
        functools.partial(_in_proj_kernel, gelu_tiles=gelu_cols // tn),
        out_shape=jax.ShapeDtypeStruct((n, width), BF16),
        grid=(n // tm, width // tn),
        in_specs=[
            pl.BlockSpec((tm, d), lambda i, j: (i, 0)),
            pl.BlockSpec((1, d), lambda i, j: (0, 0)),
            pl.BlockSpec((d, tn), lambda i, j: (0, j)),
        ],
        out_specs=pl.BlockSpec((tm, tn), lambda i, j: (i, j)),
        scratch_shapes=[pltpu.VMEM((tm, d), BF16)],
        compiler_params=pltpu.CompilerParams(
            dimension_semantics=("parallel", "arbitrary"),
            vmem_limit_bytes=VMEM_LIMIT_BYTES),
        name="in_proj",
    )(x2, g, w)


def _gmlp_kernel(u_ref, v_ref, lg_ref, lb_ref, ws_ref, bs_ref, o_ref, *, chunks):
    rows = lax.broadcasted_iota(jnp.int32, (CHUNK, CHUNK), 0)
    cols = lax.broadcasted_iota(jnp.int32, (CHUNK, CHUNK), 1)
    w = jnp.where(cols <= rows, ws_ref[0], 0.0).astype(BF16)
    lg = lg_ref[0]
    lb = lb_ref[0]
    bs = bs_ref[0]
    for c in range(chunks):
        sl = pl.ds(c * CHUNK, CHUNK)
        v = v_ref[sl, :].astype(F32)
        mu = jnp.mean(v, axis=-1, keepdims=True)
        vc = v - mu
        var = jnp.mean(vc * vc, axis=-1, keepdims=True)
        vn = vc * lax.rsqrt(var + EPS) * lg + lb
        mixed = jnp.dot(w, vn.astype(BF16), preferred_element_type=F32) + bs
        o_ref[sl, :] = (u_ref[sl, :].astype(F32) * mixed).astype(BF16)


def _gmlp_gate(proj, ln_g, ln_b, w_s, b_s, *, tm):
    n = proj.shape[0]
    return pl.pallas_call(
        functools.partial(_gmlp_kernel, chunks=tm // CHUNK),
        out_shape=jax.ShapeDtypeStruct((n, GROUPS * GROUP_CH), BF16),
        grid=(n // tm, GROUPS),
        in_specs=[
            pl.BlockSpec((tm, GROUP_CH), lambda i, g: (i, 2 * g)),
            pl.BlockSpec((tm, GROUP_CH), lambda i, g: (i, 2 * g + 1)),
            pl.BlockSpec((1, 1, GROUP_CH), lambda i, g: (g, 0, 0)),
            pl.BlockSpec((1, 1, GROUP_CH), lambda i, g: (g, 0, 0)),
            pl.BlockSpec((1, CHUNK, CHUNK), lambda i, g: (g, 0, 0)),
            pl.BlockSpec((1, CHUNK, 1), lambda i, g: (g, 0, 0)),
        ],
        out_specs=pl.BlockSpec((tm, GROUP_CH), lambda i, g: (i, g)),
        compiler_params=pltpu.CompilerParams(
            dimension_semantics=("parallel", "parallel"),
            vmem_limit_bytes=VMEM_LIMIT_BYTES),
        name="gmlp_gate",
    )(proj, proj, ln_g, ln_b, w_s, b_s)


def _attn_kernel(slopes_ref, q_ref, k_ref, v_ref, lq1_ref, lk1_ref, lq2_ref, lk2_ref,
                 g_ref, o_ref, m_ref, l_ref, acc_ref, *, t, lambda_init):
    h = pl.program_id(1)
    qi = pl.program_id(2)
    slope = slopes_ref[h]

    lane = lax.broadcasted_iota(jnp.int32, (t, 2 * HEAD_DK), 1)
    q = q_ref[...] * jnp.asarray(HEAD_DK ** -0.5, BF16)
    zero = jnp.zeros_like(q)
    qz = jnp.concatenate([jnp.where(lane < HEAD_DK, q, zero),
                          jnp.where(lane >= HEAD_DK, q, zero)], axis=0)

    m_ref[...] = jnp.full(m_ref.shape, -jnp.inf, F32)
    l_ref[...] = jnp.zeros(l_ref.shape, F32)
    acc_ref[...] = jnp.zeros(acc_ref.shape, F32)

    col = lax.broadcasted_iota(jnp.int32, (1, t), 1)

    def tile(ki, masked):
        ks = pl.ds(pl.multiple_of(ki * t, t), t)
        s = lax.dot_general(qz, k_ref[ks, :], (((1,), (1,)), ((), ())),
                            preferred_element_type=F32)
        s = s + slope * (ki * t + col).astype(F32)
        if masked:
            r = lax.broadcasted_iota(jnp.int32, (2 * t, t), 0)
            r = jnp.where(r >= t, r - t, r)
            c = lax.broadcasted_iota(jnp.int32, (2 * t, t), 1)
            s = jnp.where(c <= r, s, -jnp.inf)
        m_old = m_ref[...]
        m_new = jnp.maximum(m_old, jnp.max(s, axis=-1, keepdims=True))
        alpha = jnp.exp(m_old - m_new)
        p = jnp.exp(s - m_new)
        l_ref[...] = alpha * l_ref[...] + jnp.sum(p, axis=-1, keepdims=True)
        acc_ref[...] = alpha * acc_ref[...] + jnp.dot(
            p.astype(BF16), v_ref[ks, :], preferred_element_type=F32)
        m_ref[...] = m_new

    def body(ki, carry):
        tile(ki, masked=False)
        return carry

    lax.fori_loop(0, qi, body, 0)
    tile(qi, masked=True)

    lam = (jnp.exp(jnp.sum(lq1_ref[...] * lk1_ref[...])) -
           jnp.exp(jnp.sum(lq2_ref[...] * lk2_ref[...])) + lambda_init)
    o = acc_ref[...] / l_ref[...]
    o = o[:t] - lam * o[t:]
    o_ref[...] = (_rms(o, g_ref[...]) * (1.0 - lambda_init)).astype(BF16)


def _diff_attn(proj, slopes, lq1, lk1, lq2, lk2, g, *, batch, seq, t, q_col, k_col, v_col,
               lambda_init):
    n = proj.shape[0]
    nq = seq // t
    vec = lambda: pl.BlockSpec((1, HEAD_DK), lambda b, h, i: (0, 0))
    return pl.pallas_call(
        functools.partial(_attn_kernel, t=t, lambda_init=lambda_init),
        out_shape=jax.ShapeDtypeStruct((n, HEADS * HEAD_DV), BF16),
        grid=(batch, HEADS, nq),
        in_specs=[
            pl.BlockSpec(memory_space=pltpu.SMEM),
            pl.BlockSpec((t, LANES), lambda b, h, i: (b * nq + i, q_col + h)),
            pl.BlockSpec((seq, LANES), lambda b, h, i: (b, k_col + h)),
            pl.BlockSpec((seq, LANES), lambda b, h, i: (b, v_col + h)),
            vec(), vec(), vec(), vec(),
            pl.BlockSpec((1, HEAD_DV), lambda b, h, i: (0, 0)),
        ],
        out_specs=pl.BlockSpec((t, HEAD_DV), lambda b, h, i: (b * nq + i, h)),
        scratch_shapes=[
            pltpu.VMEM((2 * t, 1), F32),
            pltpu.VMEM((2 * t, 1), F32),
            pltpu.VMEM((2 * t, HEAD_DV), F32),
        ],
        compiler_params=pltpu.CompilerParams(
            dimension_semantics=("parallel", "parallel", "arbitrary"),
            vmem_limit_bytes=VMEM_LIMIT_BYTES),
        name="diff_attn",
    )(slopes, proj, proj, proj, lq1, lk1, lq2, lk2, g)


def _out_proj_kernel(ya_ref, yb_ref, wa_ref, wb_ref, x_ref, g_ref, o_ref):
    y = jnp.dot(ya_ref[...], wa_ref[...], preferred_element_type=F32)
    y = y + jnp.dot(yb_ref[...], wb_ref[...], preferred_element_type=F32)
    o_ref[...] = x_ref[...] + _rms(y, g_ref[...])


def _out_proj(ya, yb, w, x2, g, *, tm):
    n, d = x2.shape
    ka, kb = ya.shape[1], yb.shape[1]
    return pl.pallas_call(
        _out_proj_kernel,
        out_shape=jax.ShapeDtypeStruct((n, d), F32),
        grid=(n // tm,),
        in_specs=[
            pl.BlockSpec((tm, ka), lambda i: (i, 0)),
            pl.BlockSpec((tm, kb), lambda i: (i, 0)),
            pl.BlockSpec((ka, d), lambda i: (0, 0)),
            pl.BlockSpec((kb, d), lambda i: (1, 0)),
            pl.BlockSpec((tm, d), lambda i: (i, 0)),
            pl.BlockSpec((1, d), lambda i: (0, 0)),
        ],
        out_specs=pl.BlockSpec((tm, d), lambda i: (i, 0)),
        compiler_params=pltpu.CompilerParams(
            dimension_semantics=("parallel",),
            vmem_limit_bytes=VMEM_LIMIT_BYTES),
        name="out_proj",
    )(ya, yb, w, w, x2, g)


def _mlp_kernel(h_ref, g1_ref, wu_ref, wd_ref, g2_ref, o_ref, hn_ref, acc_ref):
    f = pl.program_id(1)

    @pl.when(f == 0)
    def _():
        hn_ref[...] = _rms(h_ref[...], g1_ref[...]).astype(BF16)
        acc_ref[...] = jnp.zeros(acc_ref.shape, F32)

    up = jnp.dot(hn_ref[...], wu_ref[...], preferred_element_type=F32)
    act = jnp.square(jnp.maximum(up, 0.0)).astype(BF16)
    acc_ref[...] += jnp.dot(act, wd_ref[...], preferred_element_type=F32)

    @pl.when(f == pl.num_programs(1) - 1)
    def _():
        o_ref[...] = h_ref[...] + _rms(acc_ref[...], g2_ref[...])


def _mlp(h1, g1, wu, wd, g2, *, tm, tf):
    n, d = h1.shape
    dff = wu.shape[1]
    return pl.pallas_call(
        _mlp_kernel,
        out_shape=jax.ShapeDtypeStruct((n, d), F32),
        grid=(n // tm, dff // tf),
        in_specs=[
            pl.BlockSpec((tm, d), lambda i, f: (i, 0)),
            pl.BlockSpec((1, d), lambda i, f: (0, 0)),
            pl.BlockSpec((d, tf), lambda i, f: (0, f)),
            pl.BlockSpec((tf, d), lambda i, f: (f, 0)),
            pl.BlockSpec((1, d), lambda i, f: (0, 0)),
        ],
        out_specs=pl.BlockSpec((tm, d), lambda i, f: (i, 0)),
        scratch_shapes=[pltpu.VMEM((tm, d), BF16), pltpu.VMEM((tm, d), F32)],
        compiler_params=pltpu.CompilerParams(
            dimension_semantics=("parallel", "arbitrary"),
            vmem_limit_bytes=VMEM_LIMIT_BYTES),
        name="mlp",
    )(h1, g1, wu, wd, g2)


def kernel(x, pre_mix_g, w_in, gmlp_ln_g, gmlp_ln_b, gmlp_w_s, gmlp_b_s, lambda_q1, lambda_k1,
           lambda_q2, lambda_k2, diff_subln_g, w_out, post_mix_g, pre_mlp_g, w_up, w_down,
           post_mlp_g):
    batch, seq, d = x.shape
    n = batch * seq
    depth = w_in.shape[0]
    gmlp_cols = 2 * GROUPS * GROUP_CH
    qk_cols = HEADS * 2 * HEAD_DK
    assert w_in.shape[2] == gmlp_cols + 2 * qk_cols + HEADS * HEAD_DV
    assert seq % CHUNK == 0

    tm_in = min(1024, n)
    tm_gate = min(1024, n)
    t_attn = min(512, seq)
    tm_out = min(512, n)
    tm_mlp = min(512, n)
    tf_mlp = min(512, w_up.shape[2])

    slopes = jnp.asarray(2.0 ** (-8.0 * np.arange(1, HEADS + 1) / HEADS), dtype=F32)
    row = lambda a: a.reshape(1, -1).astype(F32)

    h = x.reshape(n, d)
    for l in range(depth):
        lambda_init = 0.8 - 0.6 * math.exp(-0.3 * l)
        proj = _in_proj(h, row(pre_mix_g[l]), w_in[l].astype(BF16),
                        gelu_cols=gmlp_cols, tm=tm_in, tn=512)
        y_a = _gmlp_gate(proj,
                         gmlp_ln_g[l].reshape(GROUPS, 1, GROUP_CH).astype(F32),
                         gmlp_ln_b[l].reshape(GROUPS, 1, GROUP_CH).astype(F32),
                         gmlp_w_s[l].astype(F32),
                         gmlp_b_s[l].reshape(GROUPS, CHUNK, 1).astype(F32),
                         tm=tm_gate)
        y_b = _diff_attn(proj, slopes, row(lambda_q1[l]), row(lambda_k1[l]),
                         row(lambda_q2[l]), row(lambda_k2[l]), row(diff_subln_g[l]),
                         batch=batch, seq=seq, t=t_attn,
                         q_col=gmlp_cols // LANES,
                         k_col=(gmlp_cols + qk_cols) // LANES,
                         v_col=(gmlp_cols + 2 * qk_cols) // LANES,
                         lambda_init=lambda_init)
        h1 = _out_proj(y_a, y_b, w_out[l].astype(BF16), h, row(post_mix_g[l]), tm=tm_out)
        h = _mlp(h1, row(pre_mlp_g[l]), w_up[l].astype(BF16), w_down[l].astype(BF16),
                 row(post_mlp_g[l]), tm=tm_mlp, tf=tf_mlp)
    return h.reshape(batch, seq, d)
```

```python
import functools
import math

import jax
import jax.numpy as jnp
import numpy as np
from jax import lax
from jax.experimental import pallas as pl
from jax.experimental.pallas import tpu as pltpu

EPS = 1e-6
LOG2E = math.log2(math.e)
GROUPS = 8
GROUP_CH = 128
CHUNK = 128
HEADS = 8
HEAD_DV = 128
HEAD_DK = 64
LANES = 128
VMEM_LIMIT_BYTES = 56 * 1024 * 1024

F32 = jnp.float32
BF16 = jnp.bfloat16


def _rms(x, g):
    return x * lax.rsqrt(jnp.mean(x * x, axis=-1, keepdims=True) + EPS) * g


def _gelu_tanh(x):
    c = math.sqrt(2.0 / math.pi)
    return 0.5 * x * (1.0 + jnp.tanh(c * (x + 0.044715 * (x * x * x))))


def _in_proj_kernel(x_ref, g_ref, w_ref, o_ref, xn_ref, *, gelu_tiles):
    j = pl.program_id(1)

    @pl.when(j == 0)
    def _():
        xn_ref[...] = _rms(x_ref[...], g_ref[...]).astype(BF16)

    acc = jnp.dot(xn_ref[...], w_ref[...], preferred_element_type=F32)

    @pl.when(j < gelu_tiles)
    def _():
        o_ref[...] = _gelu_tanh(acc).astype(BF16)

    @pl.when(j >= gelu_tiles)
    def _():
        o_ref[...] = acc.astype(BF16)


def _in_proj(x2, g, w, *, gelu_cols, tm, tn):
    n, d = x2.shape
    width = w.shape[1]
    return pl.pallas_call(
        functools.partial(_in_proj_kernel, gelu_tiles=gelu_cols // tn),
        out_shape=jax.ShapeDtypeStruct((n, width), BF16),
        grid=(n // tm, width // tn),
        in_specs=[
            pl.BlockSpec((tm, d), lambda i, j: (i, 0)),
            pl.BlockSpec((1, d), lambda i, j: (0, 0)),
            pl.BlockSpec((d, tn), lambda i, j: (0, j)),
        ],
        out_specs=pl.BlockSpec((tm, tn), lambda i, j: (i, j)),
        scratch_shapes=[pltpu.VMEM((tm, d), BF16)],
        compiler_params=pltpu.CompilerParams(
            dimension_semantics=("parallel", "arbitrary"),
            vmem_limit_bytes=VMEM_LIMIT_BYTES),
        name="in_proj",
    )(x2, g, w)


def _gmlp_kernel(u_ref, v_ref, lg_ref, lb_ref, ws_ref, bs_ref, o_ref, *, chunks):
    rows = lax.broadcasted_iota(jnp.int32, (CHUNK, CHUNK), 0)
    cols = lax.broadcasted_iota(jnp.int32, (CHUNK, CHUNK), 1)
    w = jnp.where(cols <= rows, ws_ref[0], 0.0).astype(BF16)
    lg = lg_ref[0]
    lb = lb_ref[0]
    bs = bs_ref[0]
    for c in range(chunks):
        sl = pl.ds(c * CHUNK, CHUNK)
        v = v_ref[sl, :].astype(F32)
        mu = jnp.mean(v, axis=-1, keepdims=True)
        vc = v - mu
        var = jnp.mean(vc * vc, axis=-1, keepdims=True)
        vn = vc * lax.rsqrt(var + EPS) * lg + lb
        mixed = jnp.dot(w, vn.astype(BF16), preferred_element_type=F32) + bs
        o_ref[sl, :] = (u_ref[sl, :].astype(F32) * mixed).astype(BF16)


def _gmlp_gate(proj, ln_g, ln_b, w_s, b_s, *, tm):
    n = proj.shape[0]
    return pl.pallas_call(
        functools.partial(_gmlp_kernel, chunks=tm // CHUNK),
        out_shape=jax.ShapeDtypeStruct((n, GROUPS * GROUP_CH), BF16),
        grid=(n // tm, GROUPS),
        in_specs=[
            pl.BlockSpec((tm, GROUP_CH), lambda i, g: (i, 2 * g)),
            pl.BlockSpec((tm, GROUP_CH), lambda i, g: (i, 2 * g + 1)),
            pl.BlockSpec((1, 1, GROUP_CH), lambda i, g: (g, 0, 0)),
            pl.BlockSpec((1, 1, GROUP_CH), lambda i, g: (g, 0, 0)),
            pl.BlockSpec((1, CHUNK, CHUNK), lambda i, g: (g, 0, 0)),
            pl.BlockSpec((1, CHUNK, 1), lambda i, g: (g, 0, 0)),
        ],
        out_specs=pl.BlockSpec((tm, GROUP_CH), lambda i, g: (i, g)),
        compiler_params=pltpu.CompilerParams(
            dimension_semantics=("parallel", "parallel"),
            vmem_limit_bytes=VMEM_LIMIT_BYTES),
        name="gmlp_gate",
    )(proj, proj, ln_g, ln_b, w_s, b_s)


def _attn_kernel(qf_ref, q_ref, k_ref, v_ref, lq1_ref, lk1_ref, lq2_ref, lk2_ref,
                 g_ref, o_ref, kaug_ref, qaug_ref, s_ref, mx_ref, m_ref, l_ref, acc_ref,
                 *, t, seq, lambda_init):
    qi = pl.program_id(2)
    n_lane_tiles = t // LANES

    @pl.when(qi == 0)
    def _():
        kaug_ref[:, :LANES] = k_ref[...]
        pos = lax.broadcasted_iota(jnp.int32, (seq, LANES), 0)
        lane = lax.broadcasted_iota(jnp.int32, (seq, LANES), 1)
        feat = jnp.where(lane < 3, pos >> 6, jnp.where(lane < 6, pos & 63, 0))
        kaug_ref[:, LANES:] = feat.astype(F32).astype(BF16)

    lane = lax.broadcasted_iota(jnp.int32, (t, LANES), 1)
    q = (q_ref[...].astype(F32) * (HEAD_DK ** -0.5 * LOG2E)).astype(BF16)
    zero = jnp.zeros_like(q)
    qf = jnp.broadcast_to(qf_ref[0], (t, LANES))
    qaug_ref[:t, :LANES] = jnp.where(lane < HEAD_DK, q, zero)
    qaug_ref[t:, :LANES] = jnp.where(lane >= HEAD_DK, q, zero)
    qaug_ref[:t, LANES:] = qf
    qaug_ref[t:, LANES:] = qf

    m_ref[...] = jnp.full(m_ref.shape, -jnp.inf, F32)
    l_ref[...] = jnp.zeros(l_ref.shape, F32)
    acc_ref[...] = jnp.zeros(acc_ref.shape, F32)

    def scores(ki, slot, masked):
        ks = pl.ds(pl.multiple_of(ki * t, t), t)
        s = lax.dot_general(qaug_ref[...], kaug_ref[ks, :], (((1,), (1,)), ((), ())),
                            preferred_element_type=F32)
        if masked:
            r = lax.broadcasted_iota(jnp.int32, (2 * t, t), 0)
            r = jnp.where(r >= t, r - t, r)
            c = lax.broadcasted_iota(jnp.int32, (2 * t, t), 1)
            s = jnp.where(c <= r, s, -jnp.inf)
        s_ref[slot] = s
        mx = s[:, :LANES]
        for j in range(1, n_lane_tiles):
            mx = jnp.maximum(mx, s[:, j * LANES:(j + 1) * LANES])
        mx_ref[slot] = jnp.broadcast_to(jnp.max(mx, axis=-1, keepdims=True), (2 * t, LANES))

    def accumulate(ki, slot):
        ks = pl.ds(pl.multiple_of(ki * t, t), t)
        m_old = m_ref[...]
        m_new = jnp.maximum(m_old, mx_ref[slot])
        alpha = jnp.exp2(m_old - m_new)
        p = jnp.exp2(s_ref[slot] - jnp.tile(m_new, (1, n_lane_tiles)))
        psum = p[:, :LANES]
        for j in range(1, n_lane_tiles):
            psum = psum + p[:, j * LANES:(j + 1) * LANES]
        l_ref[...] = alpha * l_ref[...] + psum
        acc_ref[...] = alpha * acc_ref[...] + jnp.dot(
            p.astype(BF16), v_ref[ks, :], preferred_element_type=F32)
        m_ref[...] = m_new

    def step(i, masked):
        @pl.when(i % 2 == 0)
        def _():
            accumulate(i, 0)
            scores(i + 1, 1, masked)

        @pl.when(i % 2 == 1)
        def _():
            accumulate(i, 1)
            scores(i + 1, 0, masked)

    @pl.when(qi == 0)
    def _():
        scores(0, 0, True)

    @pl.when(qi > 0)
    def _():
        scores(0, 0, False)

    def body(i, carry):
        step(i, False)
        return carry

    lax.fori_loop(0, qi - 1, body, 0)

    @pl.when(qi > 0)
    def _():
        step(qi - 1, True)

    @pl.when(qi % 2 == 0)
    def _():
        accumulate(qi, 0)

    @pl.when(qi % 2 == 1)
    def _():
        accumulate(qi, 1)

    lam = (jnp.exp(jnp.sum(lq1_ref[...] * lk1_ref[...])) -
           jnp.exp(jnp.sum(lq2_ref[...] * lk2_ref[...])) + lambda_init)
    o = acc_ref[...] / jnp.sum(l_ref[...], axis=-1, keepdims=True)
    o = o[:t] - lam * o[t:]
    o_ref[...] = (_rms(o, g_ref[...]) * (1.0 - lambda_init)).astype(BF16)


def _slope_features(slopes):
    feats = np.zeros((HEADS, 1, LANES), np.float32)
    for h, slope in enumerate(slopes):
        for base, c in ((0, 64.0 * LOG2E * slope), (3, LOG2E * slope)):
            rest = np.float64(c)
            for j in range(3):
                piece = np.float64(np.float32(rest).astype(BF16))
                feats[h, 0, base + j] = piece
                rest = rest - piece
    return jnp.asarray(feats, dtype=BF16)


def _diff_attn(proj, qfeat, lq1, lk1, lq2, lk2, g, *, batch, seq, t, q_col, k_col, v_col,
               lambda_init):
    n = proj.shape[0]
    nq = seq // t
    vec = lambda: pl.BlockSpec((1, HEAD_DK), lambda b, h, i: (0, 0))
    return pl.pallas_call(
        functools.partial(_attn_kernel, t=t, seq=seq, lambda_init=lambda_init),
        out_shape=jax.ShapeDtypeStruct((n, HEADS * HEAD_DV), BF16),
        grid=(batch, HEADS, nq),
        in_specs=[
            pl.BlockSpec((1, 1, LANES), lambda b, h, i: (h, 0, 0)),
            pl.BlockSpec((t, LANES), lambda b, h, i: (b * nq + i, q_col + h)),
            pl.BlockSpec((seq, LANES), lambda b, h, i: (b, k_col + h)),
            pl.BlockSpec((seq, LANES), lambda b, h, i: (b, v_col + h)),
            vec(), vec(), vec(), vec(),
            pl.BlockSpec((1, HEAD_DV), lambda b, h, i: (0, 0)),
        ],
        out_specs=pl.BlockSpec((t, HEAD_DV), lambda b, h, i: (b * nq + i, h)),
        scratch_shapes=[
            pltpu.VMEM((seq, 2 * LANES), BF16),
            pltpu.VMEM((2 * t, 2 * LANES), BF16),
            pltpu.VMEM((2, 2 * t, t), F32),
            pltpu.VMEM((2, 2 * t, LANES), F32),
            pltpu.VMEM((2 * t, LANES), F32),
            pltpu.VMEM((2 * t, LANES), F32),
            pltpu.VMEM((2 * t, HEAD_DV), F32),
        ],
        compiler_params=pltpu.CompilerParams(
            dimension_semantics=("parallel", "parallel", "arbitrary"),
            vmem_limit_bytes=VMEM_LIMIT_BYTES),
        name="diff_attn",
    )(qfeat, proj, proj, proj, lq1, lk1, lq2, lk2, g)


def _out_proj_kernel(ya_ref, yb_ref, wa_ref, wb_ref, x_ref, g_ref, o_ref):
    y = jnp.dot(ya_ref[...], wa_ref[...], preferred_element_type=F32)
    y = y + jnp.dot(yb_ref[...], wb_ref[...], preferred_element_type=F32)
    o_ref[...] = x_ref[...] + _rms(y, g_ref[...])


def _out_proj(ya, yb, w, x2, g, *, tm):
    n, d = x2.shape
    ka, kb = ya.shape[1], yb.shape[1]
    return pl.pallas_call(
        _out_proj_kernel,
        out_shape=jax.ShapeDtypeStruct((n, d), F32),
        grid=(n // tm,),
        in_specs=[
            pl.BlockSpec((tm, ka), lambda i: (i, 0)),
            pl.BlockSpec((tm, kb), lambda i: (i, 0)),
            pl.BlockSpec((ka, d), lambda i: (0, 0)),
            pl.BlockSpec((kb, d), lambda i: (1, 0)),
            pl.BlockSpec((tm, d), lambda i: (i, 0)),
            pl.BlockSpec((1, d), lambda i: (0, 0)),
        ],
        out_specs=pl.BlockSpec((tm, d), lambda i: (i, 0)),
        compiler_params=pltpu.CompilerParams(
            dimension_semantics=("parallel",),
            vmem_limit_bytes=VMEM_LIMIT_BYTES),
        name="out_proj",
    )(ya, yb, w, w, x2, g)


def _mlp_kernel(h_ref, g1_ref, wu_ref, wd_ref, g2_ref, o_ref, hn_ref, acc_ref):
    f = pl.program_id(1)

    @pl.when(f == 0)
    def _():
        hn_ref[...] = _rms(h_ref[...], g1_ref[...]).astype(BF16)
        acc_ref[...] = jnp.zeros(acc_ref.shape, F32)

    up = jnp.dot(hn_ref[...], wu_ref[...], preferred_element_type=F32)
    act = jnp.square(jnp.maximum(up, 0.0)).astype(BF16)
    acc_ref[...] += jnp.dot(act, wd_ref[...], preferred_element_type=F32)

    @pl.when(f == pl.num_programs(1) - 1)
    def _():
        o_ref[...] = h_ref[...] + _rms(acc_ref[...], g2_ref[...])


def _mlp(h1, g1, wu, wd, g2, *, tm, tf):
    n, d = h1.shape
    dff = wu.shape[1]
    return pl.pallas_call(
        _mlp_kernel,
        out_shape=jax.ShapeDtypeStruct((n, d), F32),
        grid=(n // tm, dff // tf),
        in_specs=[
            pl.BlockSpec((tm, d), lambda i, f: (i, 0)),
            pl.BlockSpec((1, d), lambda i, f: (0, 0)),
            pl.BlockSpec((d, tf), lambda i, f: (0, f)),
            pl.BlockSpec((tf, d), lambda i, f: (f, 0)),
            pl.BlockSpec((1, d), lambda i, f: (0, 0)),
        ],
        out_specs=pl.BlockSpec((tm, d), lambda i, f: (i, 0)),
        scratch_shapes=[pltpu.VMEM((tm, d), BF16), pltpu.VMEM((tm, d), F32)],
        compiler_params=pltpu.CompilerParams(
            dimension_semantics=("parallel", "arbitrary"),
            vmem_limit_bytes=VMEM_LIMIT_BYTES),
        name="mlp",
    )(h1, g1, wu, wd, g2)


def kernel(x, pre_mix_g, w_in, gmlp_ln_g, gmlp_ln_b, gmlp_w_s, gmlp_b_s, lambda_q1, lambda_k1,
           lambda_q2, lambda_k2, diff_subln_g, w_out, post_mix_g, pre_mlp_g, w_up, w_down,
           post_mlp_g):
    batch, seq, d = x.shape
    n = batch * seq
    depth = w_in.shape[0]
    gmlp_cols = 2 * GROUPS * GROUP_CH
    qk_cols = HEADS * 2 * HEAD_DK
    assert w_in.shape[2] == gmlp_cols + 2 * qk_cols + HEADS * HEAD_DV
    assert seq % CHUNK == 0

    tm_in = min(1024, n)
    tm_gate = min(1024, n)
    t_attn = min(512, seq)
    tm_out = min(512, n)
    tm_mlp = min(512, n)
    tf_mlp = min(512, w_up.shape[2])

    qfeat = _slope_features(2.0 ** (-8.0 * np.arange(1, HEADS + 1) / HEADS))
    row = lambda a: a.reshape(1, -1).astype(F32)

    h = x.reshape(n, d)
    for l in range(depth):
        lambda_init = 0.8 - 0.6 * math.exp(-0.3 * l)
        proj = _in_proj(h, row(pre_mix_g[l]), w_in[l].astype(BF16),
                        gelu_cols=gmlp_cols, tm=tm_in, tn=512)
        y_a = _gmlp_gate(proj,
                         gmlp_ln_g[l].reshape(GROUPS, 1, GROUP_CH).astype(F32),
                         gmlp_ln_b[l].reshape(GROUPS, 1, GROUP_CH).astype(F32),
                         gmlp_w_s[l].astype(F32),
                         gmlp_b_s[l].reshape(GROUPS, CHUNK, 1).astype(F32),
                         tm=tm_gate)
        y_b = _diff_attn(proj, qfeat, row(lambda_q1[l]), row(lambda_k1[l]),
                         row(lambda_q2[l]), row(lambda_k2[l]), row(diff_subln_g[l]),
                         batch=batch, seq=seq, t=t_attn,
                         q_col=gmlp_cols // LANES,
                         k_col=(gmlp_cols + qk_cols) // LANES,
                         v_col=(gmlp_cols + 2 * qk_cols) // LANES,
                         lambda_init=lambda_init)
        h1 = _out_proj(y_a, y_b, w_out[l].astype(BF16), h, row(post_mix_g[l]), tm=tm_out)
        h = _mlp(h1, row(pre_mlp_g[l]), w_up[l].astype(BF16), w_down[l].astype(BF16),
                 row(post_mlp_g[l]), tm=tm_mlp, tf=tf_mlp)
    return h.reshape(batch, seq, d)
```

```python
import functools
import math

import jax
import jax.numpy as jnp
import numpy as np
from jax import lax
from jax.experimental import pallas as pl
from jax.experimental.pallas import tpu as pltpu

EPS = 1e-6
LOG2E = math.log2(math.e)
GROUPS = 8
GROUP_CH = 128
CHUNK = 128
HEADS = 8
HEAD_DV = 128
HEAD_DK = 64
LANES = 128
ROW_BLOCKS = 4
VMEM_LIMIT_BYTES = 56 * 1024 * 1024

F32 = jnp.float32
BF16 = jnp.bfloat16


def _rms(x, g):
    return x * lax.rsqrt(jnp.mean(x * x, axis=-1, keepdims=True) + EPS) * g


def _gelu_tanh(x):
    c = math.sqrt(2.0 / math.pi)
    return 0.5 * x * (1.0 + jnp.tanh(c * (x + 0.044715 * (x * x * x))))


def _in_proj_kernel(x_ref, g_ref, w_ref, lg_ref, lb_ref, ws_ref, bs_ref, ya_ref, qkv_ref, xn_ref,
                    *, gate_tiles, groups_per_tile, chunks):
    j = pl.program_id(1)

    def gate(xn):
        z = _gelu_tanh(jnp.dot(xn, w_ref[...], preferred_element_type=F32))
        rows = lax.broadcasted_iota(jnp.int32, (CHUNK, CHUNK), 0)
        cols = lax.broadcasted_iota(jnp.int32, (CHUNK, CHUNK), 1)
        for gl in range(groups_per_tile):
            u = z[:, 2 * gl * GROUP_CH:(2 * gl + 1) * GROUP_CH]
            v = z[:, (2 * gl + 1) * GROUP_CH:(2 * gl + 2) * GROUP_CH]
            mu = jnp.mean(v, axis=-1, keepdims=True)
            vc = v - mu
            var = jnp.mean(vc * vc, axis=-1, keepdims=True)
            vn = (vc * lax.rsqrt(var + EPS) * lg_ref[gl] + lb_ref[gl]).astype(BF16)
            w = jnp.where(cols <= rows, ws_ref[gl], 0.0).astype(BF16)
            bs = bs_ref[gl]
            for c in range(chunks):
                sl = slice(c * CHUNK, (c + 1) * CHUNK)
                mixed = jnp.dot(w, vn[sl, :], preferred_element_type=F32) + bs
                ya_ref[sl, gl * GROUP_CH:(gl + 1) * GROUP_CH] = (u[sl, :] * mixed).astype(BF16)

    @pl.when(j == 0)
    def _():
        xn = _rms(x_ref[...], g_ref[...]).astype(BF16)
        xn_ref[...] = xn
        gate(xn)

    @pl.when(jnp.logical_and(j > 0, j < gate_tiles))
    def _():
        gate(xn_ref[...])

    @pl.when(j >= gate_tiles)
    def _():
        qkv_ref[...] = jnp.dot(xn_ref[...], w_ref[...],
                               preferred_element_type=F32).astype(BF16)


def _in_proj(x2, g, w, ln_g, ln_b, w_s, b_s, *, tm, tn):
    n, d = x2.shape
    gate_cols = 2 * GROUPS * GROUP_CH
    gate_tiles = gate_cols // tn
    gpt = tn // (2 * GROUP_CH)
    qkv_cols = w.shape[1] - gate_cols
    last_gate = gate_tiles - 1
    group_block = lambda i, j: (jnp.minimum(j, last_gate), 0, 0)
    return pl.pallas_call(
        functools.partial(_in_proj_kernel, gate_tiles=gate_tiles, groups_per_tile=gpt,
                          chunks=tm // CHUNK),
        out_shape=(jax.ShapeDtypeStruct((n, GROUPS * GROUP_CH), BF16),
                   jax.ShapeDtypeStruct((n, qkv_cols), BF16)),
        grid=(n // tm, w.shape[1] // tn),
        in_specs=[
            pl.BlockSpec((tm, d), lambda i, j: (i, 0)),
            pl.BlockSpec((1, d), lambda i, j: (0, 0)),
            pl.BlockSpec((d, tn), lambda i, j: (0, j)),
            pl.BlockSpec((gpt, 1, GROUP_CH), group_block),
            pl.BlockSpec((gpt, 1, GROUP_CH), group_block),
            pl.BlockSpec((gpt, CHUNK, CHUNK), group_block),
            pl.BlockSpec((gpt, CHUNK, 1), group_block),
        ],
        out_specs=(
            pl.BlockSpec((tm, gpt * GROUP_CH), lambda i, j: (i, jnp.minimum(j, last_gate))),
            pl.BlockSpec((tm, tn), lambda i, j: (i, jnp.maximum(j - gate_tiles, 0))),
        ),
        scratch_shapes=[pltpu.VMEM((tm, d), BF16)],
        compiler_params=pltpu.CompilerParams(
            dimension_semantics=("parallel", "arbitrary"),
            vmem_limit_bytes=VMEM_LIMIT_BYTES),
        name="in_proj",
    )(x2, g, w, ln_g, ln_b, w_s, b_s)


def _attn_kernel(qf_ref, q_ref, qn_ref, k_ref, v_ref, lq1_ref, lk1_ref, lq2_ref, lk2_ref,
                 g_ref, o_ref, kaug_ref, qaug_ref, s_ref, mx_ref, m_ref, l_ref, acc_ref,
                 *, t, seq, lambda_init):
    qi = pl.program_id(2)
    n_lane_tiles = t // LANES
    rb = 2 * t // ROW_BLOCKS

    def stage_queries(qslot, q_tile):
        lane = lax.broadcasted_iota(jnp.int32, (t, LANES), 1)
        q = (q_tile.astype(F32) * (HEAD_DK ** -0.5 * LOG2E)).astype(BF16)
        zero = jnp.zeros_like(q)
        qf = jnp.broadcast_to(qf_ref[0], (t, LANES))
        qaug_ref[qslot, :t, :LANES] = jnp.where(lane < HEAD_DK, q, zero)
        qaug_ref[qslot, t:, :LANES] = jnp.where(lane >= HEAD_DK, q, zero)
        qaug_ref[qslot, :t, LANES:] = qf
        qaug_ref[qslot, t:, LANES:] = qf

    def scores_rows(qslot, ki, slot, r0, masked):
        ks = pl.ds(pl.multiple_of(ki * t, t), t)
        s = lax.dot_general(qaug_ref[qslot, r0:r0 + rb, :], kaug_ref[ks, :],
                            (((1,), (1,)), ((), ())), preferred_element_type=F32)
        if masked:
            r = (r0 % t) + lax.broadcasted_iota(jnp.int32, (rb, t), 0)
            c = lax.broadcasted_iota(jnp.int32, (rb, t), 1)
            s = jnp.where(c <= r, s, -jnp.inf)
        s_ref[slot, r0:r0 + rb, :] = s
        mx = s[:, :LANES]
        for j in range(1, n_lane_tiles):
            mx = jnp.maximum(mx, s[:, j * LANES:(j + 1) * LANES])
        mx_ref[slot, r0:r0 + rb, :] = jnp.broadcast_to(
            jnp.max(mx, axis=-1, keepdims=True), (rb, LANES))

    def accumulate_rows(ki, slot, r0):
        ks = pl.ds(pl.multiple_of(ki * t, t), t)
        rows = slice(r0, r0 + rb)
        m_old = m_ref[rows, :]
        m_new = jnp.maximum(m_old, mx_ref[slot, rows, :])
        alpha = jnp.exp2(m_old - m_new)
        p = jnp.exp2(s_ref[slot, rows, :] - jnp.tile(m_new, (1, n_lane_tiles)))
        psum = p[:, :LANES]
        for j in range(1, n_lane_tiles):
            psum = psum + p[:, j * LANES:(j + 1) * LANES]
        l_ref[rows, :] = alpha * l_ref[rows, :] + psum
        acc_ref[rows, :] = alpha * acc_ref[rows, :] + jnp.dot(
            p.astype(BF16), v_ref[ks, :], preferred_element_type=F32)
        m_ref[rows, :] = m_new

    def step(ki, parity, qslot_next, ki_next, masked_next):
        def both(slot):
            for r in range(ROW_BLOCKS):
                accumulate_rows(ki, slot, r * rb)
                scores_rows(qslot_next, ki_next, 1 - slot, r * rb, masked_next)

        @pl.when(parity == 0)
        def _():
            both(0)

        @pl.when(parity == 1)
        def _():
            both(1)

    @pl.when(qi == 0)
    def _():
        kaug_ref[:, :LANES] = k_ref[...]
        pos = lax.broadcasted_iota(jnp.int32, (seq, LANES), 0)
        lane = lax.broadcasted_iota(jnp.int32, (seq, LANES), 1)
        feat = jnp.where(lane < 3, pos >> 6, jnp.where(lane < 6, pos & 63, 0))
        kaug_ref[:, LANES:] = feat.astype(F32).astype(BF16)
        stage_queries(0, q_ref[...])
        for r in range(ROW_BLOCKS):
            scores_rows(0, 0, 0, r * rb, True)

    qslot = qi % 2
    stage_queries(1 - qslot, qn_ref[...])

    m_ref[...] = jnp.full(m_ref.shape, -jnp.inf, F32)
    l_ref[...] = jnp.zeros(l_ref.shape, F32)
    acc_ref[...] = jnp.zeros(acc_ref.shape, F32)

    base = (qi * (qi + 1) // 2) % 2

    def body(k, carry):
        step(k, (base + k) % 2, qslot, k + 1, False)
        return carry

    lax.fori_loop(0, qi - 1, body, 0)

    @pl.when(qi > 0)
    def _():
        step(qi - 1, (base + qi - 1) % 2, qslot, qi, True)

    step(qi, (base + qi) % 2, 1 - qslot, 0, False)

    lam = (jnp.exp(jnp.sum(lq1_ref[...] * lk1_ref[...])) -
           jnp.exp(jnp.sum(lq2_ref[...] * lk2_ref[...])) + lambda_init)
    o = acc_ref[...] / jnp.sum(l_ref[...], axis=-1, keepdims=True)
    o = o[:t] - lam * o[t:]
    o_ref[...] = (_rms(o, g_ref[...]) * (1.0 - lambda_init)).astype(BF16)


def _slope_features(slopes):
    feats = np.zeros((HEADS, 1, LANES), np.float32)
    for h, slope in enumerate(slopes):
        for base, c in ((0, 64.0 * LOG2E * slope), (3, LOG2E * slope)):
            rest = np.float64(c)
            for j in range(3):
                piece = np.float64(np.float32(rest).astype(BF16))
                feats[h, 0, base + j] = piece
                rest = rest - piece
    return jnp.asarray(feats, dtype=BF16)


def _diff_attn(qkv, qfeat, lq1, lk1, lq2, lk2, g, *, batch, seq, t, q_col, k_col, v_col,
               lambda_init):
    n = qkv.shape[0]
    nq = seq // t
    vec = lambda: pl.BlockSpec((1, HEAD_DK), lambda b, h, i: (0, 0))
    return pl.pallas_call(
        functools.partial(_attn_kernel, t=t, seq=seq, lambda_init=lambda_init),
        out_shape=jax.ShapeDtypeStruct((n, HEADS * HEAD_DV), BF16),
        grid=(batch, HEADS, nq),
        in_specs=[
            pl.BlockSpec((1, 1, LANES), lambda b, h, i: (h, 0, 0)),
            pl.BlockSpec((t, LANES), lambda b, h, i: (b * nq + i, q_col + h)),
            pl.BlockSpec((t, LANES),
                         lambda b, h, i: (b * nq + jnp.minimum(i + 1, nq - 1), q_col + h)),
            pl.BlockSpec((seq, LANES), lambda b, h, i: (b, k_col + h)),
            pl.BlockSpec((seq, LANES), lambda b, h, i: (b, v_col + h)),
            vec(), vec(), vec(), vec(),
            pl.BlockSpec((1, HEAD_DV), lambda b, h, i: (0, 0)),
        ],
        out_specs=pl.BlockSpec((t, HEAD_DV), lambda b, h, i: (b * nq + i, h)),
        scratch_shapes=[
            pltpu.VMEM((seq, 2 * LANES), BF16),
            pltpu.VMEM((2, 2 * t, 2 * LANES), BF16),
            pltpu.VMEM((2, 2 * t, t), F32),
            pltpu.VMEM((2, 2 * t, LANES), F32),
            pltpu.VMEM((2 * t, LANES), F32),
            pltpu.VMEM((2 * t, LANES), F32),
            pltpu.VMEM((2 * t, HEAD_DV), F32),
        ],
        compiler_params=pltpu.CompilerParams(
            dimension_semantics=("parallel", "parallel", "arbitrary"),
            vmem_limit_bytes=VMEM_LIMIT_BYTES),
        name="diff_attn",
    )(qfeat, qkv, qkv, qkv, qkv, lq1, lk1, lq2, lk2, g)


def _out_proj_kernel(ya_ref, yb_ref, wa_ref, wb_ref, x_ref, g_ref, o_ref):
    y = jnp.dot(ya_ref[...], wa_ref[...], preferred_element_type=F32)
    y = y + jnp.dot(yb_ref[...], wb_ref[...], preferred_element_type=F32)
    o_ref[...] = x_ref[...] + _rms(y, g_ref[...])


def _out_proj(ya, yb, w, x2, g, *, tm):
    n, d = x2.shape
    ka, kb = ya.shape[1], yb.shape[1]
    return pl.pallas_call(
        _out_proj_kernel,
        out_shape=jax.ShapeDtypeStruct((n, d), F32),
        grid=(n // tm,),
        in_specs=[
            pl.BlockSpec((tm, ka), lambda i: (i, 0)),
            pl.BlockSpec((tm, kb), lambda i: (i, 0)),
            pl.BlockSpec((ka, d), lambda i: (0, 0)),
            pl.BlockSpec((kb, d), lambda i: (1, 0)),
            pl.BlockSpec((tm, d), lambda i: (i, 0)),
            pl.BlockSpec((1, d), lambda i: (0, 0)),
        ],
        out_specs=pl.BlockSpec((tm, d), lambda i: (i, 0)),
        compiler_params=pltpu.CompilerParams(
            dimension_semantics=("parallel",),
            vmem_limit_bytes=VMEM_LIMIT_BYTES),
        name="out_proj",
    )(ya, yb, w, w, x2, g)


def _mlp_kernel(h_ref, g1_ref, wu_ref, wd_ref, g2_ref, o_ref, hn_ref):
    acc_ref = o_ref
    f = pl.program_id(1)
    last = pl.num_programs(1) - 1

    def down(hn):
        up = jnp.dot(hn, wu_ref[...], preferred_element_type=F32)
        act = jnp.square(jnp.maximum(up, 0.0)).astype(BF16)
        return jnp.dot(act, wd_ref[...], preferred_element_type=F32)

    @pl.when(f == 0)
    def _():
        hn = _rms(h_ref[...], g1_ref[...]).astype(BF16)
        hn_ref[...] = hn
        acc_ref[...] = down(hn)

    @pl.when(jnp.logical_and(f > 0, f < last))
    def _():
        acc_ref[...] += down(hn_ref[...])

    @pl.when(f == last)
    def _():
        y = acc_ref[...] + down(hn_ref[...])
        o_ref[...] = h_ref[...] + _rms(y, g2_ref[...])


def _mlp(h1, g1, wu, wd, g2, *, tm, tf):
    n, d = h1.shape
    dff = wu.shape[1]
    return pl.pallas_call(
        _mlp_kernel,
        out_shape=jax.ShapeDtypeStruct((n, d), F32),
        grid=(n // tm, dff // tf),
        in_specs=[
            pl.BlockSpec((tm, d), lambda i, f: (i, 0)),
            pl.BlockSpec((1, d), lambda i, f: (0, 0)),
            pl.BlockSpec((d, tf), lambda i, f: (0, f)),
            pl.BlockSpec((tf, d), lambda i, f: (f, 0)),
            pl.BlockSpec((1, d), lambda i, f: (0, 0)),
        ],
        out_specs=pl.BlockSpec((tm, d), lambda i, f: (i, 0)),
        scratch_shapes=[pltpu.VMEM((tm, d), BF16)],
        compiler_params=pltpu.CompilerParams(
            dimension_semantics=("parallel", "arbitrary"),
            vmem_limit_bytes=VMEM_LIMIT_BYTES),
        name="mlp",
    )(h1, g1, wu, wd, g2)


def kernel(x, pre_mix_g, w_in, gmlp_ln_g, gmlp_ln_b, gmlp_w_s, gmlp_b_s, lambda_q1, lambda_k1,
           lambda_q2, lambda_k2, diff_subln_g, w_out, post_mix_g, pre_mlp_g, w_up, w_down,
           post_mlp_g):
    batch, seq, d = x.shape
    n = batch * seq
    depth = w_in.shape[0]
    gmlp_cols = 2 * GROUPS * GROUP_CH
    qk_cols = HEADS * 2 * HEAD_DK
    assert w_in.shape[2] == gmlp_cols + 2 * qk_cols + HEADS * HEAD_DV
    assert seq % CHUNK == 0

    tm_in = min(1024, n)
    t_attn = min(1024, seq)
    tm_out = min(512, n)
    tm_mlp = min(1024, n)
    tf_mlp = min(512, w_up.shape[2])

    qfeat = _slope_features(2.0 ** (-8.0 * np.arange(1, HEADS + 1) / HEADS))
    row = lambda a: a.reshape(1, -1).astype(F32)

    h = x.reshape(n, d)
    for l in range(depth):
        lambda_init = 0.8 - 0.6 * math.exp(-0.3 * l)
        y_a, qkv = _in_proj(h, row(pre_mix_g[l]), w_in[l].astype(BF16),
                            gmlp_ln_g[l].reshape(GROUPS, 1, GROUP_CH).astype(F32),
                            gmlp_ln_b[l].reshape(GROUPS, 1, GROUP_CH).astype(F32),
                            gmlp_w_s[l].astype(F32),
                            gmlp_b_s[l].reshape(GROUPS, CHUNK, 1).astype(F32),
                            tm=tm_in, tn=512)
        y_b = _diff_attn(qkv, qfeat, row(lambda_q1[l]), row(lambda_k1[l]),
                         row(lambda_q2[l]), row(lambda_k2[l]), row(diff_subln_g[l]),
                         batch=batch, seq=seq, t=t_attn,
                         q_col=0, k_col=qk_cols // LANES, v_col=2 * qk_cols // LANES,
                         lambda_init=lambda_init)
        h1 = _out_proj(y_a, y_b, w_out[l].astype(BF16), h, row(post_mix_g[l]), tm=tm_out)
        h = _mlp(h1, row(pre_mlp_g[l]), w_up[l].astype(BF16), w_down[l].astype(BF16),
                 row(post_mlp_g[l]), tm=tm_mlp, tf=tf_mlp)
    return h.reshape(batch, seq, d)
```

```python
import functools
import math

import jax
import jax.numpy as jnp
import numpy as np
from jax import lax
from jax.experimental import pallas as pl
from jax.experimental.pallas import tpu as pltpu

EPS = 1e-6
LOG2E = math.log2(math.e)
GROUPS = 8
GROUP_CH = 128
CHUNK = 128
HEADS = 8
HEAD_DV = 128
HEAD_DK = 64
LANES = 128
MXU_COLS = 256
ROW_BLOCKS = 4
VMEM_LIMIT_BYTES = 56 * 1024 * 1024

F32 = jnp.float32
BF16 = jnp.bfloat16


def _rms(x, g):
    return x * lax.rsqrt(jnp.mean(x * x, axis=-1, keepdims=True) + EPS) * g


def _gelu_tanh(x):
    c = math.sqrt(2.0 / math.pi)
    return 0.5 * x * (1.0 + jnp.tanh(c * (x + 0.044715 * (x * x * x))))


def _in_proj_kernel(x_ref, g_ref, w_ref, lg_ref, lb_ref, ws_ref, bs_ref, ya_ref, qkv_ref,
                    xn_ref, z_ref, *, gate_tiles, groups_per_tile, chunks):
    j = pl.program_id(1)

    def project():
        return jnp.dot(xn_ref[...], w_ref[...], preferred_element_type=F32)

    def gate(slot):
        z = _gelu_tanh(z_ref[slot])
        rows = lax.broadcasted_iota(jnp.int32, (CHUNK, CHUNK), 0)
        cols = lax.broadcasted_iota(jnp.int32, (CHUNK, CHUNK), 1)
        for gl in range(groups_per_tile):
            u = z[:, 2 * gl * GROUP_CH:(2 * gl + 1) * GROUP_CH]
            v = z[:, (2 * gl + 1) * GROUP_CH:(2 * gl + 2) * GROUP_CH]
            mu = jnp.mean(v, axis=-1, keepdims=True)
            vc = v - mu
            var = jnp.mean(vc * vc, axis=-1, keepdims=True)
            vn = (vc * lax.rsqrt(var + EPS) * lg_ref[gl] + lb_ref[gl]).astype(BF16)
            w = jnp.where(cols <= rows, ws_ref[gl], 0.0).astype(BF16)
            bs = bs_ref[gl]
            for c in range(chunks):
                sl = slice(c * CHUNK, (c + 1) * CHUNK)
                mixed = jnp.dot(w, vn[sl, :], preferred_element_type=F32) + bs
                ya_ref[sl, gl * GROUP_CH:(gl + 1) * GROUP_CH] = (u[sl, :] * mixed).astype(BF16)

    @pl.when(j == 0)
    def _():
        xn = _rms(x_ref[...], g_ref[...]).astype(BF16)
        xn_ref[...] = xn
        z_ref[0] = jnp.dot(xn, w_ref[...], preferred_element_type=F32)

    for jj in range(1, gate_tiles):
        @pl.when(j == jj)
        def _():
            z_ref[jj % 2] = project()
            gate((jj - 1) % 2)

    @pl.when(j == gate_tiles)
    def _():
        qkv_ref[...] = project().astype(BF16)
        gate((gate_tiles - 1) % 2)

    @pl.when(j > gate_tiles)
    def _():
        qkv_ref[...] = project().astype(BF16)


def _in_proj(x2, g, w, ln_g, ln_b, w_s, b_s, *, tm, tn):
    n, d = x2.shape
    gate_cols = 2 * GROUPS * GROUP_CH
    gate_tiles = gate_cols // tn
    gpt = tn // (2 * GROUP_CH)
    qkv_cols = w.shape[1] - gate_cols
    last_gate = gate_tiles - 1
    gated = lambda j: jnp.clip(j - 1, 0, last_gate)
    group_block = lambda i, j: (gated(j), 0, 0)
    return pl.pallas_call(
        functools.partial(_in_proj_kernel, gate_tiles=gate_tiles, groups_per_tile=gpt,
                          chunks=tm // CHUNK),
        out_shape=(jax.ShapeDtypeStruct((n, GROUPS * GROUP_CH), BF16),
                   jax.ShapeDtypeStruct((n, qkv_cols), BF16)),
        grid=(n // tm, w.shape[1] // tn),
        in_specs=[
            pl.BlockSpec((tm, d), lambda i, j: (i, 0)),
            pl.BlockSpec((1, d), lambda i, j: (0, 0)),
            pl.BlockSpec((d, tn), lambda i, j: (0, j)),
            pl.BlockSpec((gpt, 1, GROUP_CH), group_block),
            pl.BlockSpec((gpt, 1, GROUP_CH), group_block),
            pl.BlockSpec((gpt, CHUNK, CHUNK), group_block),
            pl.BlockSpec((gpt, CHUNK, 1), group_block),
        ],
        out_specs=(
            pl.BlockSpec((tm, gpt * GROUP_CH), lambda i, j: (i, gated(j))),
            pl.BlockSpec((tm, tn), lambda i, j: (i, jnp.maximum(j - gate_tiles, 0))),
        ),
        scratch_shapes=[pltpu.VMEM((tm, d), BF16), pltpu.VMEM((2, tm, tn), F32)],
        compiler_params=pltpu.CompilerParams(
            dimension_semantics=("parallel", "arbitrary"),
            vmem_limit_bytes=VMEM_LIMIT_BYTES),
        name="in_proj",
    )(x2, g, w, ln_g, ln_b, w_s, b_s)


def _attn_kernel(qf_ref, q_ref, qn_ref, k_ref, v_ref, lq1_ref, lk1_ref, lq2_ref, lk2_ref,
                 g_ref, o_ref, kaug_ref, vaug_ref, qaug_ref, s_ref, mx_ref, m_ref, acc_ref,
                 *, t, seq, lambda_init):
    qi = pl.program_id(2)
    rb = 2 * t // ROW_BLOCKS

    def stage_queries(qslot, q_tile):
        lane = lax.broadcasted_iota(jnp.int32, (t, LANES), 1)
        q = (q_tile.astype(F32) * (HEAD_DK ** -0.5 * LOG2E)).astype(BF16)
        zero = jnp.zeros_like(q)
        qf = jnp.broadcast_to(qf_ref[0], (t, LANES))
        qaug_ref[qslot, :t, :LANES] = jnp.where(lane < HEAD_DK, q, zero)
        qaug_ref[qslot, t:, :LANES] = jnp.where(lane >= HEAD_DK, q, zero)
        qaug_ref[qslot, :t, LANES:] = qf
        qaug_ref[qslot, t:, LANES:] = qf

    def visible_cols(r0):
        last_row = r0 % t + rb
        return min(t, -(-last_row // MXU_COLS) * MXU_COLS)

    def scores_rows(qslot, ki, slot, r0, masked):
        nk = visible_cols(r0) if masked else t
        ks = pl.ds(pl.multiple_of(ki * t, t), nk)
        s = lax.dot_general(qaug_ref[qslot, r0:r0 + rb, :], kaug_ref[ks, :],
                            (((1,), (1,)), ((), ())), preferred_element_type=F32)
        if masked:
            r = (r0 % t) + lax.broadcasted_iota(jnp.int32, (rb, nk), 0)
            c = lax.broadcasted_iota(jnp.int32, (rb, nk), 1)
            s = jnp.where(c <= r, s, -jnp.inf)
        s_ref[slot, r0:r0 + rb, :nk] = s
        mx = s[:, :LANES]
        for j in range(1, nk // LANES):
            mx = jnp.maximum(mx, s[:, j * LANES:(j + 1) * LANES])
        mx_ref[slot, r0:r0 + rb, :] = jnp.broadcast_to(
            jnp.max(mx, axis=-1, keepdims=True), (rb, LANES))

    def accumulate_rows(ki, slot, r0, diagonal):
        nk = visible_cols(r0) if diagonal else t
        ks = pl.ds(pl.multiple_of(ki * t, t), nk)
        rows = slice(r0, r0 + rb)
        m_old = m_ref[rows, :]
        m_new = jnp.maximum(m_old, mx_ref[slot, rows, :])
        alpha = jnp.exp2(m_old - m_new)
        p = jnp.exp2(s_ref[slot, rows, :nk] - jnp.tile(m_new, (1, nk // LANES))).astype(BF16)
        acc_ref[rows, :] = jnp.tile(alpha, (1, 2)) * acc_ref[rows, :] + jnp.dot(
            p, vaug_ref[ks, :], preferred_element_type=F32)
        m_ref[rows, :] = m_new

    def step(ki, parity, qslot_next, ki_next, masked_next, diagonal):
        def both(slot):
            for r in range(ROW_BLOCKS):
                accumulate_rows(ki, slot, r * rb, diagonal)
                scores_rows(qslot_next, ki_next, 1 - slot, r * rb, masked_next)

        @pl.when(parity == 0)
        def _():
            both(0)

        @pl.when(parity == 1)
        def _():
            both(1)

    @pl.when(qi == 0)
    def _():
        kaug_ref[:, :LANES] = k_ref[...]
        pos = lax.broadcasted_iota(jnp.int32, (seq, LANES), 0)
        lane = lax.broadcasted_iota(jnp.int32, (seq, LANES), 1)
        feat = jnp.where(lane < 3, pos >> 6, jnp.where(lane < 6, pos & 63, 0))
        kaug_ref[:, LANES:] = feat.astype(F32).astype(BF16)
        vaug_ref[:, :LANES] = v_ref[...]
        vaug_ref[:, LANES:] = jnp.ones((seq, LANES), BF16)
        stage_queries(0, q_ref[...])
        for r in range(ROW_BLOCKS):
            scores_rows(0, 0, 0, r * rb, True)

    qslot = qi % 2
    stage_queries(1 - qslot, qn_ref[...])

    m_ref[...] = jnp.full(m_ref.shape, -jnp.inf, F32)
    acc_ref[...] = jnp.zeros(acc_ref.shape, F32)

    base = (qi * (qi + 1) // 2) % 2

    def body(k, carry):
        step(k, (base + k) % 2, qslot, k + 1, False, False)
        return carry

    lax.fori_loop(0, qi - 1, body, 0)

    @pl.when(qi > 0)
    def _():
        step(qi - 1, (base + qi - 1) % 2, qslot, qi, True, False)

    step(qi, (base + qi) % 2, 1 - qslot, 0, False, True)

    lam = (jnp.exp(jnp.sum(lq1_ref[...] * lk1_ref[...])) -
           jnp.exp(jnp.sum(lq2_ref[...] * lk2_ref[...])) + lambda_init)
    acc = acc_ref[...]
    o = acc[:, :HEAD_DV] / acc[:, HEAD_DV:]
    o = o[:t] - lam * o[t:]
    o_ref[...] = (_rms(o, g_ref[...]) * (1.0 - lambda_init)).astype(BF16)


def _slope_features(slopes):
    feats = np.zeros((HEADS, 1, LANES), np.float32)
    for h, slope in enumerate(slopes):
        for base, c in ((0, 64.0 * LOG2E * slope), (3, LOG2E * slope)):
            rest = np.float64(c)
            for j in range(3):
                piece = np.float64(np.float32(rest).astype(BF16))
                feats[h, 0, base + j] = piece
                rest = rest - piece
    return jnp.asarray(feats, dtype=BF16)


def _diff_attn(qkv, qfeat, lq1, lk1, lq2, lk2, g, *, batch, seq, t, q_col, k_col, v_col,
               lambda_init):
    n = qkv.shape[0]
    nq = seq // t
    vec = lambda: pl.BlockSpec((1, HEAD_DK), lambda b, h, i: (0, 0))
    return pl.pallas_call(
        functools.partial(_attn_kernel, t=t, seq=seq, lambda_init=lambda_init),
        out_shape=jax.ShapeDtypeStruct((n, HEADS * HEAD_DV), BF16),
        grid=(batch, HEADS, nq),
        in_specs=[
            pl.BlockSpec((1, 1, LANES), lambda b, h, i: (h, 0, 0)),
            pl.BlockSpec((t, LANES), lambda b, h, i: (b * nq + i, q_col + h)),
            pl.BlockSpec((t, LANES),
                         lambda b, h, i: (b * nq + jnp.minimum(i + 1, nq - 1), q_col + h)),
            pl.BlockSpec((seq, LANES), lambda b, h, i: (b, k_col + h)),
            pl.BlockSpec((seq, LANES), lambda b, h, i: (b, v_col + h)),
            vec(), vec(), vec(), vec(),
            pl.BlockSpec((1, HEAD_DV), lambda b, h, i: (0, 0)),
        ],
        out_specs=pl.BlockSpec((t, HEAD_DV), lambda b, h, i: (b * nq + i, h)),
        scratch_shapes=[
            pltpu.VMEM((seq, 2 * LANES), BF16),
            pltpu.VMEM((seq, 2 * LANES), BF16),
            pltpu.VMEM((2, 2 * t, 2 * LANES), BF16),
            pltpu.VMEM((2, 2 * t, t), F32),
            pltpu.VMEM((2, 2 * t, LANES), F32),
            pltpu.VMEM((2 * t, LANES), F32),
            pltpu.VMEM((2 * t, 2 * HEAD_DV), F32),
        ],
        compiler_params=pltpu.CompilerParams(
            dimension_semantics=("parallel", "parallel", "arbitrary"),
            vmem_limit_bytes=VMEM_LIMIT_BYTES),
        name="diff_attn",
    )(qfeat, qkv, qkv, qkv, qkv, lq1, lk1, lq2, lk2, g)


def _out_proj_kernel(ya_ref, yb_ref, wa_ref, wb_ref, x_ref, g_ref, o_ref):
    y = jnp.dot(ya_ref[...], wa_ref[...], preferred_element_type=F32)
    y = y + jnp.dot(yb_ref[...], wb_ref[...], preferred_element_type=F32)
    o_ref[...] = x_ref[...] + _rms(y, g_ref[...])


def _out_proj(ya, yb, w, x2, g, *, tm):
    n, d = x2.shape
    ka, kb = ya.shape[1], yb.shape[1]
    return pl.pallas_call(
        _out_proj_kernel,
        out_shape=jax.ShapeDtypeStruct((n, d), F32),
        grid=(n // tm,),
        in_specs=[
            pl.BlockSpec((tm, ka), lambda i: (i, 0)),
            pl.BlockSpec((tm, kb), lambda i: (i, 0)),
            pl.BlockSpec((ka, d), lambda i: (0, 0)),
            pl.BlockSpec((kb, d), lambda i: (1, 0)),
            pl.BlockSpec((tm, d), lambda i: (i, 0)),
            pl.BlockSpec((1, d), lambda i: (0, 0)),
        ],
        out_specs=pl.BlockSpec((tm, d), lambda i: (i, 0)),
        compiler_params=pltpu.CompilerParams(
            dimension_semantics=("parallel",),
            vmem_limit_bytes=VMEM_LIMIT_BYTES),
        name="out_proj",
    )(ya, yb, w, w, x2, g)


def _mlp_kernel(h_ref, g1_ref, wu_ref, wd_ref, g2_ref, o_ref, hn_ref):
    acc_ref = o_ref
    f = pl.program_id(1)
    last = pl.num_programs(1) - 1

    def down(hn):
        up = jnp.dot(hn, wu_ref[...], preferred_element_type=F32)
        act = jnp.square(jnp.maximum(up, 0.0)).astype(BF16)
        return jnp.dot(act, wd_ref[...], preferred_element_type=F32)

    @pl.when(f == 0)
    def _():
        hn = _rms(h_ref[...], g1_ref[...]).astype(BF16)
        hn_ref[...] = hn
        acc_ref[...] = down(hn)

    @pl.when(jnp.logical_and(f > 0, f < last))
    def _():
        acc_ref[...] += down(hn_ref[...])

    @pl.when(f == last)
    def _():
        y = acc_ref[...] + down(hn_ref[...])
        o_ref[...] = h_ref[...] + _rms(y, g2_ref[...])


def _mlp(h1, g1, wu, wd, g2, *, tm, tf):
    n, d = h1.shape
    dff = wu.shape[1]
    return pl.pallas_call(
        _mlp_kernel,
        out_shape=jax.ShapeDtypeStruct((n, d), F32),
        grid=(n // tm, dff // tf),
        in_specs=[
            pl.BlockSpec((tm, d), lambda i, f: (i, 0)),
            pl.BlockSpec((1, d), lambda i, f: (0, 0)),
            pl.BlockSpec((d, tf), lambda i, f: (0, f)),
            pl.BlockSpec((tf, d), lambda i, f: (f, 0)),
            pl.BlockSpec((1, d), lambda i, f: (0, 0)),
        ],
        out_specs=pl.BlockSpec((tm, d), lambda i, f: (i, 0)),
        scratch_shapes=[pltpu.VMEM((tm, d), BF16)],
        compiler_params=pltpu.CompilerParams(
            dimension_semantics=("parallel", "arbitrary"),
            vmem_limit_bytes=VMEM_LIMIT_BYTES),
        name="mlp",
    )(h1, g1, wu, wd, g2)


def kernel(x, pre_mix_g, w_in, gmlp_ln_g, gmlp_ln_b, gmlp_w_s, gmlp_b_s, lambda_q1, lambda_k1,
           lambda_q2, lambda_k2, diff_subln_g, w_out, post_mix_g, pre_mlp_g, w_up, w_down,
           post_mlp_g):
    batch, seq, d = x.shape
    n = batch * seq
    depth = w_in.shape[0]
    gmlp_cols = 2 * GROUPS * GROUP_CH
    qk_cols = HEADS * 2 * HEAD_DK
    assert w_in.shape[2] == gmlp_cols + 2 * qk_cols + HEADS * HEAD_DV
    assert seq % CHUNK == 0

    tm_in = min(1024, n)
    t_attn = min(1024, seq)
    tm_out = min(512, n)
    tm_mlp = min(1024, n)
    tf_mlp = min(512, w_up.shape[2])

    qfeat = _slope_features(2.0 ** (-8.0 * np.arange(1, HEADS + 1) / HEADS))
    row = lambda a: a.reshape(1, -1).astype(F32)

    h = x.reshape(n, d)
    for l in range(depth):
        lambda_init = 0.8 - 0.6 * math.exp(-0.3 * l)
        y_a, qkv = _in_proj(h, row(pre_mix_g[l]), w_in[l].astype(BF16),
                            gmlp_ln_g[l].reshape(GROUPS, 1, GROUP_CH).astype(F32),
                            gmlp_ln_b[l].reshape(GROUPS, 1, GROUP_CH).astype(F32),
                            gmlp_w_s[l].astype(F32),
                            gmlp_b_s[l].reshape(GROUPS, CHUNK, 1).astype(F32),
                            tm=tm_in, tn=512)
        y_b = _diff_attn(qkv, qfeat, row(lambda_q1[l]), row(lambda_k1[l]),
                         row(lambda_q2[l]), row(lambda_k2[l]), row(diff_subln_g[l]),
                         batch=batch, seq=seq, t=t_attn,
                         q_col=0, k_col=qk_cols // LANES, v_col=2 * qk_cols // LANES,
                         lambda_init=lambda_init)
        h1 = _out_proj(y_a, y_b, w_out[l].astype(BF16), h, row(post_mix_g[l]), tm=tm_out)
        h = _mlp(h1, row(pre_mlp_g[l]), w_up[l].astype(BF16), w_down[l].astype(BF16),
                 row(post_mlp_g[l]), tm=tm_mlp, tf=tf_mlp)
    return h.reshape(batch, seq, d)
```

```python
import functools
import math

import jax
import jax.numpy as jnp
import numpy as np
from jax import lax
from jax.experimental import pallas as pl
from jax.experimental.pallas import tpu as pltpu

EPS = 1e-6
LOG2E = math.log2(math.e)
GROUPS = 8
GROUP_CH = 128
CHUNK = 128
HEADS = 8
HEAD_DV = 128
HEAD_DK = 64
LANES = 128
MXU_COLS = 256
ROW_BLOCKS = 8
VMEM_LIMIT_BYTES = 56 * 1024 * 1024

F32 = jnp.float32
BF16 = jnp.bfloat16


def _rms(x, g):
    return x * lax.rsqrt(jnp.mean(x * x, axis=-1, keepdims=True) + EPS) * g


def _gelu_tanh(x):
    c = math.sqrt(2.0 / math.pi)
    return 0.5 * x * (1.0 + jnp.tanh(c * (x + 0.044715 * (x * x * x))))


def _in_proj_kernel(x_ref, g_ref, w_ref, lg_ref, lb_ref, ws_ref, bs_ref, ya_ref, qkv_ref,
                    xn_ref, z_ref, *, gate_tiles, groups_per_tile, chunks):
    j = pl.program_id(1)

    def project():
        return jnp.dot(xn_ref[...], w_ref[...], preferred_element_type=F32)

    def store_heads(y):
        for hh in range(qkv_ref.shape[0]):
            qkv_ref[hh] = y[:, hh * LANES:(hh + 1) * LANES].astype(BF16)

    def gate(slot):
        z = _gelu_tanh(z_ref[slot])
        rows = lax.broadcasted_iota(jnp.int32, (CHUNK, CHUNK), 0)
        cols = lax.broadcasted_iota(jnp.int32, (CHUNK, CHUNK), 1)
        for gl in range(groups_per_tile):
            u = z[:, 2 * gl * GROUP_CH:(2 * gl + 1) * GROUP_CH]
            v = z[:, (2 * gl + 1) * GROUP_CH:(2 * gl + 2) * GROUP_CH]
            mu = jnp.mean(v, axis=-1, keepdims=True)
            vc = v - mu
            var = jnp.mean(vc * vc, axis=-1, keepdims=True)
            vn = (vc * lax.rsqrt(var + EPS) * lg_ref[gl] + lb_ref[gl]).astype(BF16)
            w = jnp.where(cols <= rows, ws_ref[gl], 0.0).astype(BF16)
            bs = bs_ref[gl]
            for c in range(chunks):
                sl = slice(c * CHUNK, (c + 1) * CHUNK)
                mixed = jnp.dot(w, vn[sl, :], preferred_element_type=F32) + bs
                ya_ref[sl, gl * GROUP_CH:(gl + 1) * GROUP_CH] = (u[sl, :] * mixed).astype(BF16)

    @pl.when(j == 0)
    def _():
        xn = _rms(x_ref[...], g_ref[...]).astype(BF16)
        xn_ref[...] = xn
        z_ref[0] = jnp.dot(xn, w_ref[...], preferred_element_type=F32)

    for jj in range(1, gate_tiles):
        @pl.when(j == jj)
        def _():
            z_ref[jj % 2] = project()
            gate((jj - 1) % 2)

    @pl.when(j == gate_tiles)
    def _():
        store_heads(project())
        gate((gate_tiles - 1) % 2)

    @pl.when(j > gate_tiles)
    def _():
        store_heads(project())


def _col_tiles(w, tn):
    d, width = w.shape
    return jnp.stack([w[:, j * tn:(j + 1) * tn] for j in range(width // tn)])


def _in_proj(x2, g, w, ln_g, ln_b, w_s, b_s, *, tm):
    n, d = x2.shape
    n_tiles, _, tn = w.shape
    gate_cols = 2 * GROUPS * GROUP_CH
    gate_tiles = gate_cols // tn
    gpt = tn // (2 * GROUP_CH)
    qkv_cols = n_tiles * tn - gate_cols
    last_gate = gate_tiles - 1
    gated = lambda j: jnp.clip(j - 1, 0, last_gate)
    group_block = lambda i, j: (gated(j), 0, 0)
    return pl.pallas_call(
        functools.partial(_in_proj_kernel, gate_tiles=gate_tiles, groups_per_tile=gpt,
                          chunks=tm // CHUNK),
        out_shape=(jax.ShapeDtypeStruct((n, GROUPS * GROUP_CH), BF16),
                   jax.ShapeDtypeStruct((qkv_cols // LANES, n, LANES), BF16)),
        grid=(n // tm, n_tiles),
        in_specs=[
            pl.BlockSpec((tm, d), lambda i, j: (i, 0)),
            pl.BlockSpec((1, d), lambda i, j: (0, 0)),
            pl.BlockSpec((None, d, tn), lambda i, j: (j, 0, 0)),
            pl.BlockSpec((gpt, 1, GROUP_CH), group_block),
            pl.BlockSpec((gpt, 1, GROUP_CH), group_block),
            pl.BlockSpec((gpt, CHUNK, CHUNK), group_block),
            pl.BlockSpec((gpt, CHUNK, 1), group_block),
        ],
        out_specs=(
            pl.BlockSpec((tm, gpt * GROUP_CH), lambda i, j: (i, gated(j))),
            pl.BlockSpec((tn // LANES, tm, LANES),
                         lambda i, j: (jnp.maximum(j - gate_tiles, 0), i, 0)),
        ),
        scratch_shapes=[pltpu.VMEM((tm, d), BF16), pltpu.VMEM((2, tm, tn), F32)],
        compiler_params=pltpu.CompilerParams(
            dimension_semantics=("parallel", "arbitrary"),
            vmem_limit_bytes=VMEM_LIMIT_BYTES),
        name="in_proj",
    )(x2, g, w, ln_g, ln_b, w_s, b_s)


def _attn_kernel(qf_ref, q_ref, qn_ref, k_ref, v_ref, lq1_ref, lk1_ref, lq2_ref, lk2_ref,
                 g_ref, o_ref, kaug_ref, vaug_ref, qaug_ref, s_ref, mx_ref, m_ref, acc_ref,
                 *, t, seq, lambda_init):
    qi = pl.program_id(2)
    rb = 2 * t // ROW_BLOCKS

    def stage_queries(qslot, q_tile):
        lane = lax.broadcasted_iota(jnp.int32, (t, LANES), 1)
        q = (q_tile.astype(F32) * (HEAD_DK ** -0.5 * LOG2E)).astype(BF16)
        zero = jnp.zeros_like(q)
        qf = jnp.broadcast_to(qf_ref[0], (t, LANES))
        qaug_ref[qslot, :t, :LANES] = jnp.where(lane < HEAD_DK, q, zero)
        qaug_ref[qslot, t:, :LANES] = jnp.where(lane >= HEAD_DK, q, zero)
        qaug_ref[qslot, :t, LANES:] = qf
        qaug_ref[qslot, t:, LANES:] = qf

    def visible_cols(r0):
        last_row = r0 % t + rb
        return min(t, -(-last_row // MXU_COLS) * MXU_COLS)

    def scores_rows(qslot, ki, slot, r0, masked):
        nk = visible_cols(r0) if masked else t
        ks = pl.ds(pl.multiple_of(ki * t, t), nk)
        s = lax.dot_general(qaug_ref[qslot, r0:r0 + rb, :], kaug_ref[ks, :],
                            (((1,), (1,)), ((), ())), preferred_element_type=F32)
        if masked:
            r = (r0 % t) + lax.broadcasted_iota(jnp.int32, (rb, nk), 0)
            c = lax.broadcasted_iota(jnp.int32, (rb, nk), 1)
            s = jnp.where(c <= r, s, -jnp.inf)
        s_ref[slot, r0:r0 + rb, :nk] = s
        mx = s[:, :LANES]
        for j in range(1, nk // LANES):
            mx = jnp.maximum(mx, s[:, j * LANES:(j + 1) * LANES])
        mx_ref[slot, r0:r0 + rb, :] = jnp.broadcast_to(
            jnp.max(mx, axis=-1, keepdims=True), (rb, LANES))

    def accumulate_rows(ki, slot, r0, diagonal):
        nk = visible_cols(r0) if diagonal else t
        ks = pl.ds(pl.multiple_of(ki * t, t), nk)
        rows = slice(r0, r0 + rb)
        m_old = m_ref[rows, :]
        m_new = jnp.maximum(m_old, mx_ref[slot, rows, :])
        alpha = jnp.exp2(m_old - m_new)
        p = jnp.exp2(s_ref[slot, rows, :nk] - jnp.tile(m_new, (1, nk // LANES))).astype(BF16)
        acc_ref[rows, :] = jnp.tile(alpha, (1, 2)) * acc_ref[rows, :] + jnp.dot(
            p, vaug_ref[ks, :], preferred_element_type=F32)
        m_ref[rows, :] = m_new

    def step(ki, parity, qslot_next, ki_next, masked_next, diagonal):
        def both(slot):
            for r in range(ROW_BLOCKS):
                accumulate_rows(ki, slot, r * rb, diagonal)
                scores_rows(qslot_next, ki_next, 1 - slot, r * rb, masked_next)

        @pl.when(parity == 0)
        def _():
            both(0)

        @pl.when(parity == 1)
        def _():
            both(1)

    @pl.when(qi == 0)
    def _():
        kaug_ref[:, :LANES] = k_ref[...]
        pos = lax.broadcasted_iota(jnp.int32, (seq, LANES), 0)
        lane = lax.broadcasted_iota(jnp.int32, (seq, LANES), 1)
        feat = jnp.where(lane < 3, pos >> 6, jnp.where(lane < 6, pos & 63, 0))
        kaug_ref[:, LANES:] = feat.astype(F32).astype(BF16)
        vaug_ref[:, :LANES] = v_ref[...]
        vaug_ref[:, LANES:] = jnp.ones((seq, LANES), BF16)
        stage_queries(0, q_ref[...])
        for r in range(ROW_BLOCKS):
            scores_rows(0, 0, 0, r * rb, True)

    qslot = qi % 2
    stage_queries(1 - qslot, qn_ref[...])

    m_ref[...] = jnp.full(m_ref.shape, -jnp.inf, F32)
    acc_ref[...] = jnp.zeros(acc_ref.shape, F32)

    base = (qi * (qi + 1) // 2) % 2

    def body(k, carry):
        step(k, (base + k) % 2, qslot, k + 1, False, False)
        return carry

    lax.fori_loop(0, qi - 1, body, 0)

    @pl.when(qi > 0)
    def _():
        step(qi - 1, (base + qi - 1) % 2, qslot, qi, True, False)

    step(qi, (base + qi) % 2, 1 - qslot, 0, False, True)

    lam = (jnp.exp(jnp.sum(lq1_ref[...] * lk1_ref[...])) -
           jnp.exp(jnp.sum(lq2_ref[...] * lk2_ref[...])) + lambda_init)
    acc = acc_ref[...]
    o = acc[:, :HEAD_DV] / acc[:, HEAD_DV:]
    o = o[:t] - lam * o[t:]
    o_ref[...] = (_rms(o, g_ref[...]) * (1.0 - lambda_init)).astype(BF16)


def _slope_features(slopes):
    feats = np.zeros((HEADS, 1, LANES), np.float32)
    for h, slope in enumerate(slopes):
        for base, c in ((0, 64.0 * LOG2E * slope), (3, LOG2E * slope)):
            rest = np.float64(c)
            for j in range(3):
                piece = np.float64(np.float32(rest).astype(BF16))
                feats[h, 0, base + j] = piece
                rest = rest - piece
    return jnp.asarray(feats, dtype=BF16)


def _diff_attn(qkv, qfeat, lq1, lk1, lq2, lk2, g, *, batch, seq, t, q_col, k_col, v_col,
               lambda_init):
    n = qkv.shape[1]
    nq = seq // t
    vec = lambda: pl.BlockSpec((1, HEAD_DK), lambda b, h, i: (0, 0))
    return pl.pallas_call(
        functools.partial(_attn_kernel, t=t, seq=seq, lambda_init=lambda_init),
        out_shape=jax.ShapeDtypeStruct((n, HEADS * HEAD_DV), BF16),
        grid=(batch, HEADS, nq),
        in_specs=[
            pl.BlockSpec((1, 1, LANES), lambda b, h, i: (h, 0, 0)),
            pl.BlockSpec((None, t, LANES), lambda b, h, i: (q_col + h, b * nq + i, 0)),
            pl.BlockSpec((None, t, LANES),
                         lambda b, h, i: (q_col + h, b * nq + jnp.minimum(i + 1, nq - 1), 0)),
            pl.BlockSpec((None, seq, LANES), lambda b, h, i: (k_col + h, b, 0)),
            pl.BlockSpec((None, seq, LANES), lambda b, h, i: (v_col + h, b, 0)),
            vec(), vec(), vec(), vec(),
            pl.BlockSpec((1, HEAD_DV), lambda b, h, i: (0, 0)),
        ],
        out_specs=pl.BlockSpec((t, HEAD_DV), lambda b, h, i: (b * nq + i, h)),
        scratch_shapes=[
            pltpu.VMEM((seq, 2 * LANES), BF16),
            pltpu.VMEM((seq, 2 * LANES), BF16),
            pltpu.VMEM((2, 2 * t, 2 * LANES), BF16),
            pltpu.VMEM((2, 2 * t, t), F32),
            pltpu.VMEM((2, 2 * t, LANES), F32),
            pltpu.VMEM((2 * t, LANES), F32),
            pltpu.VMEM((2 * t, 2 * HEAD_DV), F32),
        ],
        compiler_params=pltpu.CompilerParams(
            dimension_semantics=("parallel", "parallel", "arbitrary"),
            vmem_limit_bytes=VMEM_LIMIT_BYTES),
        name="diff_attn",
    )(qfeat, qkv, qkv, qkv, qkv, lq1, lk1, lq2, lk2, g)


def _out_proj_kernel(ya_ref, yb_ref, wa_ref, wb_ref, x_ref, g_ref, o_ref):
    y = jnp.dot(ya_ref[...], wa_ref[...], preferred_element_type=F32)
    y = y + jnp.dot(yb_ref[...], wb_ref[...], preferred_element_type=F32)
    o_ref[...] = x_ref[...] + _rms(y, g_ref[...])


def _out_proj(ya, yb, w, x2, g, *, tm):
    n, d = x2.shape
    ka, kb = ya.shape[1], yb.shape[1]
    return pl.pallas_call(
        _out_proj_kernel,
        out_shape=jax.ShapeDtypeStruct((n, d), F32),
        grid=(n // tm,),
        in_specs=[
            pl.BlockSpec((tm, ka), lambda i: (i, 0)),
            pl.BlockSpec((tm, kb), lambda i: (i, 0)),
            pl.BlockSpec((ka, d), lambda i: (0, 0)),
            pl.BlockSpec((kb, d), lambda i: (1, 0)),
            pl.BlockSpec((tm, d), lambda i: (i, 0)),
            pl.BlockSpec((1, d), lambda i: (0, 0)),
        ],
        out_specs=pl.BlockSpec((tm, d), lambda i: (i, 0)),
        compiler_params=pltpu.CompilerParams(
            dimension_semantics=("parallel",),
            vmem_limit_bytes=VMEM_LIMIT_BYTES),
        name="out_proj",
    )(ya, yb, w, w, x2, g)


def _mlp_kernel(h_ref, g1_ref, wu_ref, wd_ref, g2_ref, o_ref, hn_ref):
    acc_ref = o_ref
    f = pl.program_id(1)
    last = pl.num_programs(1) - 1

    def down(hn):
        up = jnp.dot(hn, wu_ref[...], preferred_element_type=F32)
        act = jnp.square(jnp.maximum(up, 0.0)).astype(BF16)
        return jnp.dot(act, wd_ref[...], preferred_element_type=F32)

    @pl.when(f == 0)
    def _():
        hn = _rms(h_ref[...], g1_ref[...]).astype(BF16)
        hn_ref[...] = hn
        acc_ref[...] = down(hn)

    @pl.when(jnp.logical_and(f > 0, f < last))
    def _():
        acc_ref[...] += down(hn_ref[...])

    @pl.when(f == last)
    def _():
        y = acc_ref[...] + down(hn_ref[...])
        o_ref[...] = h_ref[...] + _rms(y, g2_ref[...])


def _mlp(h1, g1, wu, wd, g2, *, tm):
    n, d = h1.shape
    n_tiles, _, tf = wu.shape
    return pl.pallas_call(
        _mlp_kernel,
        out_shape=jax.ShapeDtypeStruct((n, d), F32),
        grid=(n // tm, n_tiles),
        in_specs=[
            pl.BlockSpec((tm, d), lambda i, f: (i, 0)),
            pl.BlockSpec((1, d), lambda i, f: (0, 0)),
            pl.BlockSpec((None, d, tf), lambda i, f: (f, 0, 0)),
            pl.BlockSpec((tf, d), lambda i, f: (f, 0)),
            pl.BlockSpec((1, d), lambda i, f: (0, 0)),
        ],
        out_specs=pl.BlockSpec((tm, d), lambda i, f: (i, 0)),
        scratch_shapes=[pltpu.VMEM((tm, d), BF16)],
        compiler_params=pltpu.CompilerParams(
            dimension_semantics=("parallel", "arbitrary"),
            vmem_limit_bytes=VMEM_LIMIT_BYTES),
        name="mlp",
    )(h1, g1, wu, wd, g2)


def kernel(x, pre_mix_g, w_in, gmlp_ln_g, gmlp_ln_b, gmlp_w_s, gmlp_b_s, lambda_q1, lambda_k1,
           lambda_q2, lambda_k2, diff_subln_g, w_out, post_mix_g, pre_mlp_g, w_up, w_down,
           post_mlp_g):
    batch, seq, d = x.shape
    n = batch * seq
    depth = w_in.shape[0]
    gmlp_cols = 2 * GROUPS * GROUP_CH
    qk_cols = HEADS * 2 * HEAD_DK
    assert w_in.shape[2] == gmlp_cols + 2 * qk_cols + HEADS * HEAD_DV
    assert seq % CHUNK == 0

    tm_in = min(1024, n)
    tn_in = 512
    t_attn = min(1024, seq)
    tm_out = min(512, n)
    tm_mlp = min(1024, n)
    tf_mlp = min(512, w_up.shape[2])

    qfeat = _slope_features(2.0 ** (-8.0 * np.arange(1, HEADS + 1) / HEADS))
    row = lambda a: a.reshape(1, -1).astype(F32)

    h = x.reshape(n, d)
    for l in range(depth):
        lambda_init = 0.8 - 0.6 * math.exp(-0.3 * l)
        y_a, qkv = _in_proj(h, row(pre_mix_g[l]), _col_tiles(w_in[l].astype(BF16), tn_in),
                            gmlp_ln_g[l].reshape(GROUPS, 1, GROUP_CH).astype(F32),
                            gmlp_ln_b[l].reshape(GROUPS, 1, GROUP_CH).astype(F32),
                            gmlp_w_s[l].astype(F32),
                            gmlp_b_s[l].reshape(GROUPS, CHUNK, 1).astype(F32),
                            tm=tm_in)
        y_b = _diff_attn(qkv, qfeat, row(lambda_q1[l]), row(lambda_k1[l]),
                         row(lambda_q2[l]), row(lambda_k2[l]), row(diff_subln_g[l]),
                         batch=batch, seq=seq, t=t_attn,
                         q_col=0, k_col=qk_cols // LANES, v_col=2 * qk_cols // LANES,
                         lambda_init=lambda_init)
        h1 = _out_proj(y_a, y_b, w_out[l].astype(BF16), h, row(post_mix_g[l]), tm=tm_out)
        h = _mlp(h1, row(pre_mlp_g[l]), _col_tiles(w_up[l].astype(BF16), tf_mlp),
                 w_down[l].astype(BF16), row(post_mlp_g[l]), tm=tm_mlp)
    return h.reshape(batch, seq, d)
```

```python
import functools
import math

import jax
import jax.numpy as jnp
import numpy as np
from jax import lax
from jax.experimental import pallas as pl
from jax.experimental.pallas import tpu as pltpu

EPS = 1e-6
LOG2E = math.log2(math.e)
GROUPS = 8
GROUP_CH = 128
CHUNK = 128
HEADS = 8
HEAD_DV = 128
HEAD_DK = 64
LANES = 128
MXU_COLS = 256
ROW_BLOCKS = 8
VMEM_LIMIT_BYTES = 56 * 1024 * 1024

F32 = jnp.float32
BF16 = jnp.bfloat16


def _rms(x, g):
    return x * lax.rsqrt(jnp.mean(x * x, axis=-1, keepdims=True) + EPS) * g


def _gelu_tanh(x):
    c = math.sqrt(2.0 / math.pi)
    return 0.5 * x * (1.0 + jnp.tanh(c * (x + 0.044715 * (x * x * x))))


def _in_proj_kernel(x_ref, g_ref, w_ref, lg_ref, lb_ref, ws_ref, bs_ref, ya_ref, qkv_ref,
                    xn_ref, z_ref, *, gate_tiles, groups_per_tile, chunks):
    j = pl.program_id(1)

    def project():
        return jnp.dot(xn_ref[...], w_ref[...], preferred_element_type=F32)

    def store_heads(y):
        for hh in range(qkv_ref.shape[0]):
            qkv_ref[hh] = y[:, hh * LANES:(hh + 1) * LANES].astype(BF16)

    def gate(slot):
        z = _gelu_tanh(z_ref[slot])
        rows = lax.broadcasted_iota(jnp.int32, (CHUNK, CHUNK), 0)
        cols = lax.broadcasted_iota(jnp.int32, (CHUNK, CHUNK), 1)
        for gl in range(groups_per_tile):
            u = z[:, 2 * gl * GROUP_CH:(2 * gl + 1) * GROUP_CH]
            v = z[:, (2 * gl + 1) * GROUP_CH:(2 * gl + 2) * GROUP_CH]
            mu = jnp.mean(v, axis=-1, keepdims=True)
            vc = v - mu
            var = jnp.mean(vc * vc, axis=-1, keepdims=True)
            vn = (vc * lax.rsqrt(var + EPS) * lg_ref[gl] + lb_ref[gl]).astype(BF16)
            w = jnp.where(cols <= rows, ws_ref[gl], 0.0).astype(BF16)
            bs = bs_ref[gl]
            for c in range(chunks):
                sl = slice(c * CHUNK, (c + 1) * CHUNK)
                mixed = jnp.dot(w, vn[sl, :], preferred_element_type=F32) + bs
                ya_ref[sl, gl * GROUP_CH:(gl + 1) * GROUP_CH] = (u[sl, :] * mixed).astype(BF16)

    @pl.when(j == 0)
    def _():
        xn = _rms(x_ref[...], g_ref[...]).astype(BF16)
        xn_ref[...] = xn
        z_ref[0] = jnp.dot(xn, w_ref[...], preferred_element_type=F32)

    for jj in range(1, gate_tiles):
        @pl.when(j == jj)
        def _():
            z_ref[jj % 2] = project()
            gate((jj - 1) % 2)

    @pl.when(j == gate_tiles)
    def _():
        store_heads(project())
        gate((gate_tiles - 1) % 2)

    @pl.when(j > gate_tiles)
    def _():
        store_heads(project())


def _in_proj(x2, g, w, ln_g, ln_b, w_s, b_s, *, tm, tn):
    n, d = x2.shape
    n_tiles = w.shape[1] // tn
    gate_cols = 2 * GROUPS * GROUP_CH
    gate_tiles = gate_cols // tn
    gpt = tn // (2 * GROUP_CH)
    qkv_cols = n_tiles * tn - gate_cols
    last_gate = gate_tiles - 1
    gated = lambda j: jnp.clip(j - 1, 0, last_gate)
    group_block = lambda i, j: (gated(j), 0, 0)
    return pl.pallas_call(
        functools.partial(_in_proj_kernel, gate_tiles=gate_tiles, groups_per_tile=gpt,
                          chunks=tm // CHUNK),
        out_shape=(jax.ShapeDtypeStruct((n, GROUPS * GROUP_CH), BF16),
                   jax.ShapeDtypeStruct((qkv_cols // LANES, n, LANES), BF16)),
        grid=(n // tm, n_tiles),
        in_specs=[
            pl.BlockSpec((tm, d), lambda i, j: (i, 0)),
            pl.BlockSpec((1, d), lambda i, j: (0, 0)),
            pl.BlockSpec((d, tn), lambda i, j: (0, j)),
            pl.BlockSpec((gpt, 1, GROUP_CH), group_block),
            pl.BlockSpec((gpt, 1, GROUP_CH), group_block),
            pl.BlockSpec((gpt, CHUNK, CHUNK), group_block),
            pl.BlockSpec((gpt, CHUNK, 1), group_block),
        ],
        out_specs=(
            pl.BlockSpec((tm, gpt * GROUP_CH), lambda i, j: (i, gated(j))),
            pl.BlockSpec((tn // LANES, tm, LANES),
                         lambda i, j: (jnp.maximum(j - gate_tiles, 0), i, 0)),
        ),
        scratch_shapes=[pltpu.VMEM((tm, d), BF16), pltpu.VMEM((2, tm, tn), F32)],
        compiler_params=pltpu.CompilerParams(
            dimension_semantics=("parallel", "arbitrary"),
            vmem_limit_bytes=VMEM_LIMIT_BYTES),
        name="in_proj",
    )(x2, g, w, ln_g, ln_b, w_s, b_s)


def _attn_kernel(qf_ref, q_ref, qn_ref, k_ref, v_ref, lq1_ref, lk1_ref, lq2_ref, lk2_ref,
                 g_ref, o_ref, kaug_ref, vaug_ref, qaug_ref, s_ref, mx_ref, m_ref, acc_ref, lam_ref,
                 *, t, seq, lambda_init):
    qi = pl.program_id(2)
    rb = 2 * t // ROW_BLOCKS

    def stage_queries(qslot, q_tile):
        lane = lax.broadcasted_iota(jnp.int32, (t, LANES), 1)
        q = (q_tile.astype(F32) * (HEAD_DK ** -0.5 * LOG2E)).astype(BF16)
        zero = jnp.zeros_like(q)
        qf = jnp.broadcast_to(qf_ref[0], (t, LANES))
        qaug_ref[qslot, :t, :LANES] = jnp.where(lane < HEAD_DK, q, zero)
        qaug_ref[qslot, t:, :LANES] = jnp.where(lane >= HEAD_DK, q, zero)
        qaug_ref[qslot, :t, LANES:] = qf
        qaug_ref[qslot, t:, LANES:] = qf

    def visible_cols(r0):
        last_row = r0 % t + rb
        return min(t, -(-last_row // MXU_COLS) * MXU_COLS)

    def scores_rows(qslot, ki, slot, r0, masked):
        nk = visible_cols(r0) if masked else t
        ks = pl.ds(pl.multiple_of(ki * t, t), nk)
        s = lax.dot_general(qaug_ref[qslot, r0:r0 + rb, :], kaug_ref[ks, :],
                            (((1,), (1,)), ((), ())), preferred_element_type=F32)
        if masked:
            r = (r0 % t) + lax.broadcasted_iota(jnp.int32, (rb, nk), 0)
            c = lax.broadcasted_iota(jnp.int32, (rb, nk), 1)
            s = jnp.where(c <= r, s, -jnp.inf)
        s_ref[slot, r0:r0 + rb, :nk] = s
        mx = s[:, :LANES]
        for j in range(1, nk // LANES):
            mx = jnp.maximum(mx, s[:, j * LANES:(j + 1) * LANES])
        mx_ref[slot, r0:r0 + rb, :] = jnp.broadcast_to(
            jnp.max(mx, axis=-1, keepdims=True), (rb, LANES))

    def accumulate_rows(ki, slot, r0, diagonal):
        nk = visible_cols(r0) if diagonal else t
        ks = pl.ds(pl.multiple_of(ki * t, t), nk)
        rows = slice(r0, r0 + rb)
        m_old = m_ref[rows, :]
        m_new = jnp.maximum(m_old, mx_ref[slot, rows, :])
        alpha = jnp.exp2(m_old - m_new)
        p = jnp.exp2(s_ref[slot, rows, :nk] - jnp.tile(m_new, (1, nk // LANES))).astype(BF16)
        acc_ref[rows, :] = jnp.tile(alpha, (1, 2)) * acc_ref[rows, :] + jnp.dot(
            p, vaug_ref[ks, :], preferred_element_type=F32)
        m_ref[rows, :] = m_new

    def finish_rows(r0, lam):
        rows1, rows2 = slice(r0, r0 + rb), slice(t + r0, t + r0 + rb)
        acc1, acc2 = acc_ref[rows1, :], acc_ref[rows2, :]
        o = acc1[:, :HEAD_DV] / acc1[:, HEAD_DV:] - lam * (acc2[:, :HEAD_DV] / acc2[:, HEAD_DV:])
        o_ref[rows1, :] = (_rms(o, g_ref[...]) * (1.0 - lambda_init)).astype(BF16)
        for rows in (rows1, rows2):
            m_ref[rows, :] = jnp.full((rb, LANES), -jnp.inf, F32)
            acc_ref[rows, :] = jnp.zeros((rb, 2 * HEAD_DV), F32)

    def step(ki, parity, qslot_next, ki_next, masked_next):
        def both(slot):
            for r in range(ROW_BLOCKS):
                accumulate_rows(ki, slot, r * rb, False)
                scores_rows(qslot_next, ki_next, 1 - slot, r * rb, masked_next)

        @pl.when(parity == 0)
        def _():
            both(0)

        @pl.when(parity == 1)
        def _():
            both(1)

    def last_step(ki, parity, qslot_next):
        lam = lam_ref[...]
        half = ROW_BLOCKS // 2

        def both(slot):
            stage_queries(qslot_next, qn_ref[...])
            for r in range(half):
                accumulate_rows(ki, slot, r * rb, True)
                accumulate_rows(ki, slot, (r + half) * rb, True)
                finish_rows(r * rb, lam)
                scores_rows(qslot_next, 0, 1 - slot, 2 * r * rb, False)
                scores_rows(qslot_next, 0, 1 - slot, (2 * r + 1) * rb, False)

        @pl.when(parity == 0)
        def _():
            both(0)

        @pl.when(parity == 1)
        def _():
            both(1)

    @pl.when(qi == 0)
    def _():
        kaug_ref[:, :LANES] = k_ref[...]
        pos = lax.broadcasted_iota(jnp.int32, (seq, LANES), 0)
        lane = lax.broadcasted_iota(jnp.int32, (seq, LANES), 1)
        feat = jnp.where(lane < 3, pos >> 6, jnp.where(lane < 6, pos & 63, 0))
        kaug_ref[:, LANES:] = feat.astype(F32).astype(BF16)
        vaug_ref[:, :LANES] = v_ref[...]
        vaug_ref[:, LANES:] = jnp.ones((seq, LANES), BF16)
        stage_queries(0, q_ref[...])
        for r in range(ROW_BLOCKS):
            scores_rows(0, 0, 0, r * rb, True)
        m_ref[...] = jnp.full(m_ref.shape, -jnp.inf, F32)
        acc_ref[...] = jnp.zeros(acc_ref.shape, F32)
        lam = (jnp.exp(jnp.sum(lq1_ref[...] * lk1_ref[...], keepdims=True)) -
               jnp.exp(jnp.sum(lq2_ref[...] * lk2_ref[...], keepdims=True)) + lambda_init)
        lam_ref[...] = jnp.broadcast_to(lam, lam_ref.shape)

    qslot = qi % 2
    base = (qi * (qi + 1) // 2) % 2

    def body(k, carry):
        step(k, (base + k) % 2, qslot, k + 1, False)
        return carry

    lax.fori_loop(0, qi - 1, body, 0)

    @pl.when(qi > 0)
    def _():
        step(qi - 1, (base + qi - 1) % 2, qslot, qi, True)

    last_step(qi, (base + qi) % 2, 1 - qslot)


def _slope_features(slopes):
    feats = np.zeros((HEADS, 1, LANES), np.float32)
    for h, slope in enumerate(slopes):
        for base, c in ((0, 64.0 * LOG2E * slope), (3, LOG2E * slope)):
            rest = np.float64(c)
            for j in range(3):
                piece = np.float64(np.float32(rest).astype(BF16))
                feats[h, 0, base + j] = piece
                rest = rest - piece
    return jnp.asarray(feats, dtype=BF16)


def _diff_attn(qkv, qfeat, lq1, lk1, lq2, lk2, g, *, batch, seq, t, q_col, k_col, v_col,
               lambda_init):
    n = qkv.shape[1]
    nq = seq // t
    vec = lambda: pl.BlockSpec((1, HEAD_DK), lambda b, h, i: (0, 0))
    return pl.pallas_call(
        functools.partial(_attn_kernel, t=t, seq=seq, lambda_init=lambda_init),
        out_shape=jax.ShapeDtypeStruct((n, HEADS * HEAD_DV), BF16),
        grid=(batch, HEADS, nq),
        in_specs=[
            pl.BlockSpec((1, 1, LANES), lambda b, h, i: (h, 0, 0)),
            pl.BlockSpec((None, t, LANES), lambda b, h, i: (q_col + h, b * nq + i, 0)),
            pl.BlockSpec((None, t, LANES),
                         lambda b, h, i: (q_col + h, b * nq + jnp.minimum(i + 1, nq - 1), 0)),
            pl.BlockSpec((None, seq, LANES), lambda b, h, i: (k_col + h, b, 0)),
            pl.BlockSpec((None, seq, LANES), lambda b, h, i: (v_col + h, b, 0)),
            vec(), vec(), vec(), vec(),
            pl.BlockSpec((1, HEAD_DV), lambda b, h, i: (0, 0)),
        ],
        out_specs=pl.BlockSpec((t, HEAD_DV), lambda b, h, i: (b * nq + i, h)),
        scratch_shapes=[
            pltpu.VMEM((seq, 2 * LANES), BF16),
            pltpu.VMEM((seq, 2 * LANES), BF16),
            pltpu.VMEM((2, 2 * t, 2 * LANES), BF16),
            pltpu.VMEM((2, 2 * t, t), F32),
            pltpu.VMEM((2, 2 * t, LANES), F32),
            pltpu.VMEM((2 * t, LANES), F32),
            pltpu.VMEM((2 * t, 2 * HEAD_DV), F32),
            pltpu.VMEM((1, HEAD_DV), F32),
        ],
        compiler_params=pltpu.CompilerParams(
            dimension_semantics=("parallel", "parallel", "arbitrary"),
            vmem_limit_bytes=VMEM_LIMIT_BYTES),
        name="diff_attn",
    )(qfeat, qkv, qkv, qkv, qkv, lq1, lk1, lq2, lk2, g)


def _out_proj_kernel(ya_ref, yb_ref, wa_ref, wb_ref, x_ref, g_ref, o_ref):
    y = jnp.dot(ya_ref[...], wa_ref[...], preferred_element_type=F32)
    y = y + jnp.dot(yb_ref[...], wb_ref[...], preferred_element_type=F32)
    o_ref[...] = x_ref[...] + _rms(y, g_ref[...])


def _out_proj(ya, yb, w, x2, g, *, tm):
    n, d = x2.shape
    ka, kb = ya.shape[1], yb.shape[1]
    return pl.pallas_call(
        _out_proj_kernel,
        out_shape=jax.ShapeDtypeStruct((n, d), F32),
        grid=(n // tm,),
        in_specs=[
            pl.BlockSpec((tm, ka), lambda i: (i, 0)),
            pl.BlockSpec((tm, kb), lambda i: (i, 0)),
            pl.BlockSpec((ka, d), lambda i: (0, 0)),
            pl.BlockSpec((kb, d), lambda i: (1, 0)),
            pl.BlockSpec((tm, d), lambda i: (i, 0)),
            pl.BlockSpec((1, d), lambda i: (0, 0)),
        ],
        out_specs=pl.BlockSpec((tm, d), lambda i: (i, 0)),
        compiler_params=pltpu.CompilerParams(
            dimension_semantics=("parallel",),
            vmem_limit_bytes=VMEM_LIMIT_BYTES),
        name="out_proj",
    )(ya, yb, w, w, x2, g)


def _mlp_kernel(h_ref, g1_ref, wu_ref, wd_ref, g2_ref, o_ref, hn_ref):
    acc_ref = o_ref
    f = pl.program_id(1)
    last = pl.num_programs(1) - 1

    def down(hn):
        up = jnp.dot(hn, wu_ref[...], preferred_element_type=F32)
        act = jnp.square(jnp.maximum(up, 0.0)).astype(BF16)
        return jnp.dot(act, wd_ref[...], preferred_element_type=F32)

    @pl.when(f == 0)
    def _():
        hn = _rms(h_ref[...], g1_ref[...]).astype(BF16)
        hn_ref[...] = hn
        acc_ref[...] = down(hn)

    @pl.when(jnp.logical_and(f > 0, f < last))
    def _():
        acc_ref[...] += down(hn_ref[...])

    @pl.when(f == last)
    def _():
        y = acc_ref[...] + down(hn_ref[...])
        o_ref[...] = h_ref[...] + _rms(y, g2_ref[...])


def _mlp(h1, g1, wu, wd, g2, *, tm, tf):
    n, d = h1.shape
    n_tiles = wu.shape[1] // tf
    return pl.pallas_call(
        _mlp_kernel,
        out_shape=jax.ShapeDtypeStruct((n, d), F32),
        grid=(n // tm, n_tiles),
        in_specs=[
            pl.BlockSpec((tm, d), lambda i, f: (i, 0)),
            pl.BlockSpec((1, d), lambda i, f: (0, 0)),
            pl.BlockSpec((d, tf), lambda i, f: (0, f)),
            pl.BlockSpec((tf, d), lambda i, f: (f, 0)),
            pl.BlockSpec((1, d), lambda i, f: (0, 0)),
        ],
        out_specs=pl.BlockSpec((tm, d), lambda i, f: (i, 0)),
        scratch_shapes=[pltpu.VMEM((tm, d), BF16)],
        compiler_params=pltpu.CompilerParams(
            dimension_semantics=("parallel", "arbitrary"),
            vmem_limit_bytes=VMEM_LIMIT_BYTES),
        name="mlp",
    )(h1, g1, wu, wd, g2)


def kernel(x, pre_mix_g, w_in, gmlp_ln_g, gmlp_ln_b, gmlp_w_s, gmlp_b_s, lambda_q1, lambda_k1,
           lambda_q2, lambda_k2, diff_subln_g, w_out, post_mix_g, pre_mlp_g, w_up, w_down,
           post_mlp_g):
    batch, seq, d = x.shape
    n = batch * seq
    depth = w_in.shape[0]
    gmlp_cols = 2 * GROUPS * GROUP_CH
    qk_cols = HEADS * 2 * HEAD_DK
    assert w_in.shape[2] == gmlp_cols + 2 * qk_cols + HEADS * HEAD_DV
    assert seq % CHUNK == 0

    tm_in = min(1024, n)
    tn_in = 512
    t_attn = min(1024, seq)
    tm_out = min(512, n)
    tm_mlp = min(1024, n)
    tf_mlp = min(512, w_up.shape[2])

    qfeat = _slope_features(2.0 ** (-8.0 * np.arange(1, HEADS + 1) / HEADS))
    row = lambda a: a.reshape(1, -1).astype(F32)

    h = x.reshape(n, d)
    for l in range(depth):
        lambda_init = 0.8 - 0.6 * math.exp(-0.3 * l)
        y_a, qkv = _in_proj(h, row(pre_mix_g[l]), w_in[l].astype(BF16),
                            gmlp_ln_g[l].reshape(GROUPS, 1, GROUP_CH).astype(F32),
                            gmlp_ln_b[l].reshape(GROUPS, 1, GROUP_CH).astype(F32),
                            gmlp_w_s[l].astype(F32),
                            gmlp_b_s[l].reshape(GROUPS, CHUNK, 1).astype(F32),
                            tm=tm_in, tn=tn_in)
        y_b = _diff_attn(qkv, qfeat, row(lambda_q1[l]), row(lambda_k1[l]),
                         row(lambda_q2[l]), row(lambda_k2[l]), row(diff_subln_g[l]),
                         batch=batch, seq=seq, t=t_attn,
                         q_col=0, k_col=qk_cols // LANES, v_col=2 * qk_cols // LANES,
                         lambda_init=lambda_init)
        h1 = _out_proj(y_a, y_b, w_out[l].astype(BF16), h, row(post_mix_g[l]), tm=tm_out)
        h = _mlp(h1, row(pre_mlp_g[l]), w_up[l].astype(BF16), w_down[l].astype(BF16),
                 row(post_mlp_g[l]), tm=tm_mlp, tf=tf_mlp)
    return h.reshape(batch, seq, d)
```

```python
import functools
import math

import jax
import jax.numpy as jnp
import numpy as np
from jax import lax
from jax.experimental import pallas as pl
from jax.experimental.pallas import tpu as pltpu

EPS = 1e-6
LOG2E = math.log2(math.e)
GROUPS = 8
GROUP_CH = 128
CHUNK = 128
HEADS = 8
HEAD_DV = 128
HEAD_DK = 64
LANES = 128
MXU_COLS = 256
ROW_BLOCKS = 8
VMEM_LIMIT_BYTES = 56 * 1024 * 1024

F32 = jnp.float32
BF16 = jnp.bfloat16


def _rms(x, g):
    return x * lax.rsqrt(jnp.mean(x * x, axis=-1, keepdims=True) + EPS) * g


def _gelu_tanh(x):
    c = math.sqrt(2.0 / math.pi)
    return 0.5 * x * (1.0 + jnp.tanh(c * (x + 0.044715 * (x * x * x))))


def _in_proj_kernel(xa_ref, xb_ref, g_ref, w_ref, lg_ref, lb_ref, ws_ref, bs_ref, ya_ref, qkv_ref,
                    xn_ref, z_ref, *, gate_tiles, groups_per_tile, chunks):
    j = pl.program_id(1)

    def project():
        return jnp.dot(xn_ref[...], w_ref[...], preferred_element_type=F32)

    def store_heads(y):
        for hh in range(qkv_ref.shape[0]):
            qkv_ref[hh] = y[:, hh * LANES:(hh + 1) * LANES].astype(BF16)

    def gate(slot):
        z = _gelu_tanh(z_ref[slot])
        rows = lax.broadcasted_iota(jnp.int32, (CHUNK, CHUNK), 0)
        cols = lax.broadcasted_iota(jnp.int32, (CHUNK, CHUNK), 1)
        for gl in range(groups_per_tile):
            u = z[:, 2 * gl * GROUP_CH:(2 * gl + 1) * GROUP_CH]
            v = z[:, (2 * gl + 1) * GROUP_CH:(2 * gl + 2) * GROUP_CH]
            mu = jnp.mean(v, axis=-1, keepdims=True)
            vc = v - mu
            var = jnp.mean(vc * vc, axis=-1, keepdims=True)
            vn = (vc * lax.rsqrt(var + EPS) * lg_ref[gl] + lb_ref[gl]).astype(BF16)
            w = jnp.where(cols <= rows, ws_ref[gl], 0.0).astype(BF16)
            bs = bs_ref[gl]
            for c in range(chunks):
                sl = slice(c * CHUNK, (c + 1) * CHUNK)
                mixed = jnp.dot(w, vn[sl, :], preferred_element_type=F32) + bs
                ya_ref[sl, gl * GROUP_CH:(gl + 1) * GROUP_CH] = (u[sl, :] * mixed).astype(BF16)

    @pl.when(j == 0)
    def _():
        xn = jnp.concatenate([_rms(xa_ref[...], g_ref[...]).astype(BF16),
                              _rms(xb_ref[...], g_ref[...]).astype(BF16)], axis=0)
        xn_ref[...] = xn
        z_ref[0] = jnp.dot(xn, w_ref[...], preferred_element_type=F32)

    for jj in range(1, gate_tiles):
        @pl.when(j == jj)
        def _():
            z_ref[jj % 2] = project()
            gate((jj - 1) % 2)

    @pl.when(j == gate_tiles)
    def _():
        store_heads(project())
        gate((gate_tiles - 1) % 2)

    @pl.when(j > gate_tiles)
    def _():
        store_heads(project())


def _in_proj(x2, g, w, ln_g, ln_b, w_s, b_s, *, tm, tn):
    n, d = x2.shape
    n_tiles = w.shape[1] // tn
    gate_cols = 2 * GROUPS * GROUP_CH
    gate_tiles = gate_cols // tn
    gpt = tn // (2 * GROUP_CH)
    qkv_cols = n_tiles * tn - gate_cols
    last_gate = gate_tiles - 1
    gated = lambda j: jnp.clip(j - 1, 0, last_gate)
    group_block = lambda i, j: (gated(j), 0, 0)
    last_half = 2 * (n // tm) - 1

    def x_half(half, switch_step):
        def index(i, j):
            tile = i + jnp.where(j >= switch_step, 1, 0)
            return (jnp.minimum(2 * tile + half, last_half - 1 + half), 0)
        return pl.BlockSpec((tm // 2, d), index)

    return pl.pallas_call(
        functools.partial(_in_proj_kernel, gate_tiles=gate_tiles, groups_per_tile=gpt,
                          chunks=tm // CHUNK),
        out_shape=(jax.ShapeDtypeStruct((n, GROUPS * GROUP_CH), BF16),
                   jax.ShapeDtypeStruct((qkv_cols // LANES, n, LANES), BF16)),
        grid=(n // tm, n_tiles),
        in_specs=[
            x_half(0, n_tiles // 3),
            x_half(1, 2 * n_tiles // 3),
            pl.BlockSpec((1, d), lambda i, j: (0, 0)),
            pl.BlockSpec((d, tn), lambda i, j: (0, j)),
            pl.BlockSpec((gpt, 1, GROUP_CH), group_block),
            pl.BlockSpec((gpt, 1, GROUP_CH), group_block),
            pl.BlockSpec((gpt, CHUNK, CHUNK), group_block),
            pl.BlockSpec((gpt, CHUNK, 1), group_block),
        ],
        out_specs=(
            pl.BlockSpec((tm, gpt * GROUP_CH), lambda i, j: (i, gated(j))),
            pl.BlockSpec((tn // LANES, tm, LANES),
                         lambda i, j: (jnp.maximum(j - gate_tiles, 0), i, 0)),
        ),
        scratch_shapes=[pltpu.VMEM((tm, d), BF16), pltpu.VMEM((2, tm, tn), F32)],
        compiler_params=pltpu.CompilerParams(
            dimension_semantics=("parallel", "arbitrary"),
            vmem_limit_bytes=VMEM_LIMIT_BYTES),
        name="in_proj",
    )(x2, x2, g, w, ln_g, ln_b, w_s, b_s)


def _attn_kernel(qf_ref, q_ref, qn_ref, k_ref, v_ref, lq1_ref, lk1_ref, lq2_ref, lk2_ref,
                 g_ref, o_ref, kaug_ref, vaug_ref, qaug_ref, s_ref, mx_ref, m_ref, acc_ref, lam_ref,
                 *, t, seq, lambda_init):
    qi = pl.program_id(2)
    rb = 2 * t // ROW_BLOCKS

    def stage_queries(qslot, q_tile):
        lane = lax.broadcasted_iota(jnp.int32, (t, LANES), 1)
        q = (q_tile.astype(F32) * (HEAD_DK ** -0.5 * LOG2E)).astype(BF16)
        zero = jnp.zeros_like(q)
        qf = jnp.broadcast_to(qf_ref[0], (t, LANES))
        qaug_ref[qslot, :t, :LANES] = jnp.where(lane < HEAD_DK, q, zero)
        qaug_ref[qslot, t:, :LANES] = jnp.where(lane >= HEAD_DK, q, zero)
        qaug_ref[qslot, :t, LANES:] = qf
        qaug_ref[qslot, t:, LANES:] = qf

    def visible_cols(r0):
        last_row = r0 % t + rb
        return min(t, -(-last_row // MXU_COLS) * MXU_COLS)

    def scores_rows(qslot, ki, slot, r0, masked):
        nk = visible_cols(r0) if masked else t
        ks = pl.ds(pl.multiple_of(ki * t, t), nk)
        s = lax.dot_general(qaug_ref[qslot, r0:r0 + rb, :], kaug_ref[ks, :],
                            (((1,), (1,)), ((), ())), preferred_element_type=F32)
        if masked:
            r = (r0 % t) + lax.broadcasted_iota(jnp.int32, (rb, nk), 0)
            c = lax.broadcasted_iota(jnp.int32, (rb, nk), 1)
            s = jnp.where(c <= r, s, -jnp.inf)
        s_ref[slot, r0:r0 + rb, :nk] = s
        mx = s[:, :LANES]
        for j in range(1, nk // LANES):
            mx = jnp.maximum(mx, s[:, j * LANES:(j + 1) * LANES])
        mx_ref[slot, r0:r0 + rb, :] = jnp.broadcast_to(
            jnp.max(mx, axis=-1, keepdims=True), (rb, LANES))

    def accumulate_rows(ki, slot, r0, diagonal):
        nk = visible_cols(r0) if diagonal else t
        ks = pl.ds(pl.multiple_of(ki * t, t), nk)
        rows = slice(r0, r0 + rb)
        m_old = m_ref[rows, :]
        m_new = jnp.maximum(m_old, mx_ref[slot, rows, :])
        alpha = jnp.exp2(m_old - m_new)
        p = jnp.exp2(s_ref[slot, rows, :nk] - jnp.tile(m_new, (1, nk // LANES))).astype(BF16)
        acc_ref[rows, :] = jnp.tile(alpha, (1, 2)) * acc_ref[rows, :] + jnp.dot(
            p, vaug_ref[ks, :], preferred_element_type=F32)
        m_ref[rows, :] = m_new

    def finish_rows(r0, lam):
        rows1, rows2 = slice(r0, r0 + rb), slice(t + r0, t + r0 + rb)
        acc1, acc2 = acc_ref[rows1, :], acc_ref[rows2, :]
        o = acc1[:, :HEAD_DV] / acc1[:, HEAD_DV:] - lam * (acc2[:, :HEAD_DV] / acc2[:, HEAD_DV:])
        o_ref[rows1, :] = (_rms(o, g_ref[...]) * (1.0 - lambda_init)).astype(BF16)
        for rows in (rows1, rows2):
            m_ref[rows, :] = jnp.full((rb, LANES), -jnp.inf, F32)
            acc_ref[rows, :] = jnp.zeros((rb, 2 * HEAD_DV), F32)

    def step(ki, parity, qslot_next, ki_next, masked_next):
        def both(slot):
            for r in range(ROW_BLOCKS):
                accumulate_rows(ki, slot, r * rb, False)
                scores_rows(qslot_next, ki_next, 1 - slot, r * rb, masked_next)

        @pl.when(parity == 0)
        def _():
            both(0)

        @pl.when(parity == 1)
        def _():
            both(1)

    def last_step(ki, parity, qslot_next):
        lam = lam_ref[...]
        half = ROW_BLOCKS // 2

        def both(slot):
            stage_queries(qslot_next, qn_ref[...])
            for r in range(half):
                accumulate_rows(ki, slot, r * rb, True)
                accumulate_rows(ki, slot, (r + half) * rb, True)
                finish_rows(r * rb, lam)
                scores_rows(qslot_next, 0, 1 - slot, 2 * r * rb, False)
                scores_rows(qslot_next, 0, 1 - slot, (2 * r + 1) * rb, False)

        @pl.when(parity == 0)
        def _():
            both(0)

        @pl.when(parity == 1)
        def _():
            both(1)

    @pl.when(qi == 0)
    def _():
        kaug_ref[:, :LANES] = k_ref[...]
        pos = lax.broadcasted_iota(jnp.int32, (seq, LANES), 0)
        lane = lax.broadcasted_iota(jnp.int32, (seq, LANES), 1)
        feat = jnp.where(lane < 3, pos >> 6, jnp.where(lane < 6, pos & 63, 0))
        kaug_ref[:, LANES:] = feat.astype(F32).astype(BF16)
        vaug_ref[:, :LANES] = v_ref[...]
        vaug_ref[:, LANES:] = jnp.ones((seq, LANES), BF16)
        stage_queries(0, q_ref[...])
        for r in range(ROW_BLOCKS):
            scores_rows(0, 0, 0, r * rb, True)
        m_ref[...] = jnp.full(m_ref.shape, -jnp.inf, F32)
        acc_ref[...] = jnp.zeros(acc_ref.shape, F32)
        lam = (jnp.exp(jnp.sum(lq1_ref[...] * lk1_ref[...], keepdims=True)) -
               jnp.exp(jnp.sum(lq2_ref[...] * lk2_ref[...], keepdims=True)) + lambda_init)
        lam_ref[...] = jnp.broadcast_to(lam, lam_ref.shape)

    qslot = qi % 2
    base = (qi * (qi + 1) // 2) % 2

    def two_steps(k, parity):
        def both(slot):
            for r in range(ROW_BLOCKS):
                accumulate_rows(k, slot, r * rb, False)
                scores_rows(qslot, k + 1, 1 - slot, r * rb, False)
            for r in range(ROW_BLOCKS):
                accumulate_rows(k + 1, 1 - slot, r * rb, False)
                scores_rows(qslot, k + 2, slot, r * rb, False)

        @pl.when(parity == 0)
        def _():
            both(0)

        @pl.when(parity == 1)
        def _():
            both(1)

    n_loop = jnp.maximum(qi - 1, 0)

    def body(j, carry):
        two_steps(2 * j, base)
        return carry

    lax.fori_loop(0, n_loop // 2, body, 0)

    @pl.when(n_loop % 2 == 1)
    def _():
        step(n_loop - 1, (base + n_loop - 1) % 2, qslot, n_loop, False)

    @pl.when(qi > 0)
    def _():
        step(qi - 1, (base + qi - 1) % 2, qslot, qi, True)

    last_step(qi, (base + qi) % 2, 1 - qslot)


def _slope_features(slopes):
    feats = np.zeros((HEADS, 1, LANES), np.float32)
    for h, slope in enumerate(slopes):
        for base, c in ((0, 64.0 * LOG2E * slope), (3, LOG2E * slope)):
            rest = np.float64(c)
            for j in range(3):
                piece = np.float64(np.float32(rest).astype(BF16))
                feats[h, 0, base + j] = piece
                rest = rest - piece
    return jnp.asarray(feats, dtype=BF16)


def _diff_attn(qkv, qfeat, lq1, lk1, lq2, lk2, g, *, batch, seq, t, q_col, k_col, v_col,
               lambda_init):
    n = qkv.shape[1]
    nq = seq // t
    vec = lambda: pl.BlockSpec((1, HEAD_DK), lambda b, h, i: (0, 0))
    return pl.pallas_call(
        functools.partial(_attn_kernel, t=t, seq=seq, lambda_init=lambda_init),
        out_shape=jax.ShapeDtypeStruct((n, HEADS * HEAD_DV), BF16),
        grid=(batch, HEADS, nq),
        in_specs=[
            pl.BlockSpec((1, 1, LANES), lambda b, h, i: (h, 0, 0)),
            pl.BlockSpec((None, t, LANES), lambda b, h, i: (q_col + h, b * nq + i, 0)),
            pl.BlockSpec((None, t, LANES),
                         lambda b, h, i: (q_col + h, b * nq + jnp.minimum(i + 1, nq - 1), 0)),
            pl.BlockSpec((None, seq, LANES), lambda b, h, i: (k_col + h, b, 0)),
            pl.BlockSpec((None, seq, LANES), lambda b, h, i: (v_col + h, b, 0)),
            vec(), vec(), vec(), vec(),
            pl.BlockSpec((1, HEAD_DV), lambda b, h, i: (0, 0)),
        ],
        out_specs=pl.BlockSpec((t, HEAD_DV), lambda b, h, i: (b * nq + i, h)),
        scratch_shapes=[
            pltpu.VMEM((seq, 2 * LANES), BF16),
            pltpu.VMEM((seq, 2 * LANES), BF16),
            pltpu.VMEM((2, 2 * t, 2 * LANES), BF16),
            pltpu.VMEM((2, 2 * t, t), F32),
            pltpu.VMEM((2, 2 * t, LANES), F32),
            pltpu.VMEM((2 * t, LANES), F32),
            pltpu.VMEM((2 * t, 2 * HEAD_DV), F32),
            pltpu.VMEM((1, HEAD_DV), F32),
        ],
        compiler_params=pltpu.CompilerParams(
            dimension_semantics=("parallel", "parallel", "arbitrary"),
            vmem_limit_bytes=VMEM_LIMIT_BYTES),
        name="diff_attn",
    )(qfeat, qkv, qkv, qkv, qkv, lq1, lk1, lq2, lk2, g)


def _out_proj_kernel(ya_ref, yb_ref, wa_ref, wb_ref, x_ref, g_ref, o_ref):
    y = jnp.dot(ya_ref[...], wa_ref[...], preferred_element_type=F32)
    y = y + jnp.dot(yb_ref[...], wb_ref[...], preferred_element_type=F32)
    o_ref[...] = x_ref[...] + _rms(y, g_ref[...])


def _out_proj(ya, yb, w, x2, g, *, tm):
    n, d = x2.shape
    ka, kb = ya.shape[1], yb.shape[1]
    return pl.pallas_call(
        _out_proj_kernel,
        out_shape=jax.ShapeDtypeStruct((n, d), F32),
        grid=(n // tm,),
        in_specs=[
            pl.BlockSpec((tm, ka), lambda i: (i, 0)),
            pl.BlockSpec((tm, kb), lambda i: (i, 0)),
            pl.BlockSpec((ka, d), lambda i: (0, 0)),
            pl.BlockSpec((kb, d), lambda i: (1, 0)),
            pl.BlockSpec((tm, d), lambda i: (i, 0)),
            pl.BlockSpec((1, d), lambda i: (0, 0)),
        ],
        out_specs=pl.BlockSpec((tm, d), lambda i: (i, 0)),
        compiler_params=pltpu.CompilerParams(
            dimension_semantics=("parallel",),
            vmem_limit_bytes=VMEM_LIMIT_BYTES),
        name="out_proj",
    )(ya, yb, w, w, x2, g)


def _mlp_kernel(h_ref, g1_ref, wu_ref, wd_ref, g2_ref, o_ref, hn_ref):
    acc_ref = o_ref
    f = pl.program_id(1)
    last = pl.num_programs(1) - 1

    def down(hn):
        up = jnp.dot(hn, wu_ref[...], preferred_element_type=F32)
        act = jnp.square(jnp.maximum(up, 0.0)).astype(BF16)
        return jnp.dot(act, wd_ref[...], preferred_element_type=F32)

    @pl.when(f == 0)
    def _():
        hn = _rms(h_ref[...], g1_ref[...]).astype(BF16)
        hn_ref[...] = hn
        acc_ref[...] = down(hn)

    @pl.when(jnp.logical_and(f > 0, f < last))
    def _():
        acc_ref[...] += down(hn_ref[...])

    @pl.when(f == last)
    def _():
        y = acc_ref[...] + down(hn_ref[...])
        o_ref[...] = h_ref[...] + _rms(y, g2_ref[...])


def _mlp(h1, g1, wu, wd, g2, *, tm, tf):
    n, d = h1.shape
    n_tiles = wu.shape[1] // tf
    return pl.pallas_call(
        _mlp_kernel,
        out_shape=jax.ShapeDtypeStruct((n, d), F32),
        grid=(n // tm, n_tiles),
        in_specs=[
            pl.BlockSpec((tm, d), lambda i, f: (i, 0)),
            pl.BlockSpec((1, d), lambda i, f: (0, 0)),
            pl.BlockSpec((d, tf), lambda i, f: (0, f)),
            pl.BlockSpec((tf, d), lambda i, f: (f, 0)),
            pl.BlockSpec((1, d), lambda i, f: (0, 0)),
        ],
        out_specs=pl.BlockSpec((tm, d), lambda i, f: (i, 0)),
        scratch_shapes=[pltpu.VMEM((tm, d), BF16)],
        compiler_params=pltpu.CompilerParams(
            dimension_semantics=("parallel", "arbitrary"),
            vmem_limit_bytes=VMEM_LIMIT_BYTES),
        name="mlp",
    )(h1, g1, wu, wd, g2)


def kernel(x, pre_mix_g, w_in, gmlp_ln_g, gmlp_ln_b, gmlp_w_s, gmlp_b_s, lambda_q1, lambda_k1,
           lambda_q2, lambda_k2, diff_subln_g, w_out, post_mix_g, pre_mlp_g, w_up, w_down,
           post_mlp_g):
    batch, seq, d = x.shape
    n = batch * seq
    depth = w_in.shape[0]
    gmlp_cols = 2 * GROUPS * GROUP_CH
    qk_cols = HEADS * 2 * HEAD_DK
    assert w_in.shape[2] == gmlp_cols + 2 * qk_cols + HEADS * HEAD_DV
    assert seq % CHUNK == 0

    tm_in = min(1024, n)
    tn_in = 512
    t_attn = min(1024, seq)
    tm_out = min(512, n)
    tm_mlp = min(1024, n)
    tf_mlp = min(512, w_up.shape[2])

    qfeat = _slope_features(2.0 ** (-8.0 * np.arange(1, HEADS + 1) / HEADS))
    row = lambda a: a.reshape(1, -1).astype(F32)

    h = x.reshape(n, d)
    for l in range(depth):
        lambda_init = 0.8 - 0.6 * math.exp(-0.3 * l)
        y_a, qkv = _in_proj(h, row(pre_mix_g[l]), w_in[l].astype(BF16),
                            gmlp_ln_g[l].reshape(GROUPS, 1, GROUP_CH).astype(F32),
                            gmlp_ln_b[l].reshape(GROUPS, 1, GROUP_CH).astype(F32),
                            gmlp_w_s[l].astype(F32),
                            gmlp_b_s[l].reshape(GROUPS, CHUNK, 1).astype(F32),
                            tm=tm_in, tn=tn_in)
        y_b = _diff_attn(qkv, qfeat, row(lambda_q1[l]), row(lambda_k1[l]),
                         row(lambda_q2[l]), row(lambda_k2[l]), row(diff_subln_g[l]),
                         batch=batch, seq=seq, t=t_attn,
                         q_col=0, k_col=qk_cols // LANES, v_col=2 * qk_cols // LANES,
                         lambda_init=lambda_init)
        h1 = _out_proj(y_a, y_b, w_out[l].astype(BF16), h, row(post_mix_g[l]), tm=tm_out)
        h = _mlp(h1, row(pre_mlp_g[l]), w_up[l].astype(BF16), w_down[l].astype(BF16),
                 row(post_mlp_g[l]), tm=tm_mlp, tf=tf_mlp)
    return h.reshape(batch, seq, d)
```

```python
import functools
import math

import jax
import jax.numpy as jnp
import numpy as np
from jax import lax
from jax.experimental import pallas as pl
from jax.experimental.pallas import tpu as pltpu

EPS = 1e-6
LOG2E = math.log2(math.e)
GROUPS = 8
GROUP_CH = 128
CHUNK = 128
HEADS = 8
HEAD_DV = 128
HEAD_DK = 64
LANES = 128
MXU_COLS = 256
ROW_BLOCKS = 8
X_PARTS = 4
VMEM_LIMIT_BYTES = 56 * 1024 * 1024

F32 = jnp.float32
BF16 = jnp.bfloat16


def _rms(x, g):
    return x * lax.rsqrt(jnp.mean(x * x, axis=-1, keepdims=True) + EPS) * g


def _gelu_tanh(x):
    c = math.sqrt(2.0 / math.pi)
    return 0.5 * x * (1.0 + jnp.tanh(c * (x + 0.044715 * (x * x * x))))


def _in_proj_kernel(*refs, gate_tiles, groups_per_tile, chunks):
    x_refs = refs[:X_PARTS]
    g_ref, w_ref, lg_ref, lb_ref, ws_ref, bs_ref, ya_ref, qkv_ref, xn_ref, z_ref = refs[X_PARTS:]
    j = pl.program_id(1)

    def project():
        return jnp.dot(xn_ref[...], w_ref[...], preferred_element_type=F32)

    def store_heads(y):
        for hh in range(qkv_ref.shape[0]):
            qkv_ref[hh] = y[:, hh * LANES:(hh + 1) * LANES].astype(BF16)

    def gate(slot):
        z = _gelu_tanh(z_ref[slot])
        rows = lax.broadcasted_iota(jnp.int32, (CHUNK, CHUNK), 0)
        cols = lax.broadcasted_iota(jnp.int32, (CHUNK, CHUNK), 1)
        for gl in range(groups_per_tile):
            u = z[:, 2 * gl * GROUP_CH:(2 * gl + 1) * GROUP_CH]
            v = z[:, (2 * gl + 1) * GROUP_CH:(2 * gl + 2) * GROUP_CH]
            mu = jnp.mean(v, axis=-1, keepdims=True)
            vc = v - mu
            var = jnp.mean(vc * vc, axis=-1, keepdims=True)
            vn = (vc * lax.rsqrt(var + EPS) * lg_ref[gl] + lb_ref[gl]).astype(BF16)
            w = jnp.where(cols <= rows, ws_ref[gl], 0.0).astype(BF16)
            bs = bs_ref[gl]
            for c in range(chunks):
                sl = slice(c * CHUNK, (c + 1) * CHUNK)
                mixed = jnp.dot(w, vn[sl, :], preferred_element_type=F32) + bs
                ya_ref[sl, gl * GROUP_CH:(gl + 1) * GROUP_CH] = (u[sl, :] * mixed).astype(BF16)

    @pl.when(j == 0)
    def _():
        xn = jnp.concatenate([_rms(x_ref[...], g_ref[...]).astype(BF16) for x_ref in x_refs],
                             axis=0)
        xn_ref[...] = xn
        z_ref[0] = jnp.dot(xn, w_ref[...], preferred_element_type=F32)

    for jj in range(1, gate_tiles):
        @pl.when(j == jj)
        def _():
            z_ref[jj % 2] = project()
            gate((jj - 1) % 2)

    @pl.when(j == gate_tiles)
    def _():
        store_heads(project())
        gate((gate_tiles - 1) % 2)

    @pl.when(j > gate_tiles)
    def _():
        store_heads(project())


def _in_proj(x2, g, w, ln_g, ln_b, w_s, b_s, *, tm, tn):
    n, d = x2.shape
    n_tiles = w.shape[1] // tn
    gate_cols = 2 * GROUPS * GROUP_CH
    gate_tiles = gate_cols // tn
    gpt = tn // (2 * GROUP_CH)
    qkv_cols = n_tiles * tn - gate_cols
    last_gate = gate_tiles - 1
    gated = lambda j: jnp.clip(j - 1, 0, last_gate)
    group_block = lambda i, j: (gated(j), 0, 0)
    last_tile = n // tm - 1

    def x_part(part, switch_step):
        def index(i, j):
            tile = jnp.minimum(i + jnp.where(j >= switch_step, 1, 0), last_tile)
            return (X_PARTS * tile + part, 0)
        return pl.BlockSpec((tm // X_PARTS, d), index)

    return pl.pallas_call(
        functools.partial(_in_proj_kernel, gate_tiles=gate_tiles, groups_per_tile=gpt,
                          chunks=tm // CHUNK),
        out_shape=(jax.ShapeDtypeStruct((n, GROUPS * GROUP_CH), BF16),
                   jax.ShapeDtypeStruct((qkv_cols // LANES, n, LANES), BF16)),
        grid=(n // tm, n_tiles),
        in_specs=[
            *[x_part(p, (p + 1) * n_tiles // (X_PARTS + 1)) for p in range(X_PARTS)],
            pl.BlockSpec((1, d), lambda i, j: (0, 0)),
            pl.BlockSpec((d, tn), lambda i, j: (0, j)),
            pl.BlockSpec((gpt, 1, GROUP_CH), group_block),
            pl.BlockSpec((gpt, 1, GROUP_CH), group_block),
            pl.BlockSpec((gpt, CHUNK, CHUNK), group_block),
            pl.BlockSpec((gpt, CHUNK, 1), group_block),
        ],
        out_specs=(
            pl.BlockSpec((tm, gpt * GROUP_CH), lambda i, j: (i, gated(j))),
            pl.BlockSpec((tn // LANES, tm, LANES),
                         lambda i, j: (jnp.maximum(j - gate_tiles, 0), i, 0)),
        ),
        scratch_shapes=[pltpu.VMEM((tm, d), BF16), pltpu.VMEM((2, tm, tn), F32)],
        compiler_params=pltpu.CompilerParams(
            dimension_semantics=("parallel", "arbitrary"),
            vmem_limit_bytes=VMEM_LIMIT_BYTES),
        name="in_proj",
    )(*([x2] * X_PARTS), g, w, ln_g, ln_b, w_s, b_s)


def _attn_kernel(qf_ref, q_ref, qn_ref, k_ref, v_ref, lq1_ref, lk1_ref, lq2_ref, lk2_ref,
                 g_ref, o_ref, kaug_ref, vaug_ref, qaug_ref, s_ref, mx_ref, m_ref, acc_ref, lam_ref,
                 *, t, seq, lambda_init):
    qi = pl.program_id(2)
    rb = 2 * t // ROW_BLOCKS

    def stage_queries(qslot, q_tile):
        lane = lax.broadcasted_iota(jnp.int32, (t, LANES), 1)
        q = (q_tile.astype(F32) * (HEAD_DK ** -0.5 * LOG2E)).astype(BF16)
        zero = jnp.zeros_like(q)
        qf = jnp.broadcast_to(qf_ref[0], (t, LANES))
        qaug_ref[qslot, :t, :LANES] = jnp.where(lane < HEAD_DK, q, zero)
        qaug_ref[qslot, t:, :LANES] = jnp.where(lane >= HEAD_DK, q, zero)
        qaug_ref[qslot, :t, LANES:] = qf
        qaug_ref[qslot, t:, LANES:] = qf

    def visible_cols(r0):
        last_row = r0 % t + rb
        return min(t, -(-last_row // MXU_COLS) * MXU_COLS)

    def scores_rows(qslot, ki, slot, r0, masked):
        nk = visible_cols(r0) if masked else t
        ks = pl.ds(pl.multiple_of(ki * t, t), nk)
        s = lax.dot_general(qaug_ref[qslot, r0:r0 + rb, :], kaug_ref[ks, :],
                            (((1,), (1,)), ((), ())), preferred_element_type=F32)
        if masked:
            r = (r0 % t) + lax.broadcasted_iota(jnp.int32, (rb, nk), 0)
            c = lax.broadcasted_iota(jnp.int32, (rb, nk), 1)
            s = jnp.where(c <= r, s, -jnp.inf)
        s_ref[slot, r0:r0 + rb, :nk] = s
        mx = s[:, :LANES]
        for j in range(1, nk // LANES):
            mx = jnp.maximum(mx, s[:, j * LANES:(j + 1) * LANES])
        mx_ref[slot, r0:r0 + rb, :] = jnp.broadcast_to(
            jnp.max(mx, axis=-1, keepdims=True), (rb, LANES))

    def accumulate_rows(ki, slot, r0, diagonal):
        nk = visible_cols(r0) if diagonal else t
        ks = pl.ds(pl.multiple_of(ki * t, t), nk)
        rows = slice(r0, r0 + rb)
        m_old = m_ref[rows, :]
        m_new = jnp.maximum(m_old, mx_ref[slot, rows, :])
        alpha = jnp.exp2(m_old - m_new)
        p = jnp.exp2(s_ref[slot, rows, :nk] - jnp.tile(m_new, (1, nk // LANES))).astype(BF16)
        acc_ref[rows, :] = jnp.tile(alpha, (1, 2)) * acc_ref[rows, :] + jnp.dot(
            p, vaug_ref[ks, :], preferred_element_type=F32)
        m_ref[rows, :] = m_new

    def finish_rows(r0, lam):
        rows1, rows2 = slice(r0, r0 + rb), slice(t + r0, t + r0 + rb)
        acc1, acc2 = acc_ref[rows1, :], acc_ref[rows2, :]
        o = acc1[:, :HEAD_DV] / acc1[:, HEAD_DV:] - lam * (acc2[:, :HEAD_DV] / acc2[:, HEAD_DV:])
        o_ref[rows1, :] = (_rms(o, g_ref[...]) * (1.0 - lambda_init)).astype(BF16)
        for rows in (rows1, rows2):
            m_ref[rows, :] = jnp.full((rb, LANES), -jnp.inf, F32)
            acc_ref[rows, :] = jnp.zeros((rb, 2 * HEAD_DV), F32)

    def step(ki, parity, qslot_next, ki_next, masked_next):
        def both(slot):
            for r in range(ROW_BLOCKS):
                accumulate_rows(ki, slot, r * rb, False)
                scores_rows(qslot_next, ki_next, 1 - slot, r * rb, masked_next)

        @pl.when(parity == 0)
        def _():
            both(0)

        @pl.when(parity == 1)
        def _():
            both(1)

    def last_step(ki, parity, qslot_next, has_next):
        lam = lam_ref[...]
        half = ROW_BLOCKS // 2

        def both(slot):
            if has_next:
                stage_queries(qslot_next, qn_ref[...])
            for r in range(half):
                accumulate_rows(ki, slot, r * rb, True)
                accumulate_rows(ki, slot, (r + half) * rb, True)
                finish_rows(r * rb, lam)
                if has_next:
                    scores_rows(qslot_next, 0, 1 - slot, 2 * r * rb, False)
                    scores_rows(qslot_next, 0, 1 - slot, (2 * r + 1) * rb, False)

        @pl.when(parity == 0)
        def _():
            both(0)

        @pl.when(parity == 1)
        def _():
            both(1)

    @pl.when(qi == 0)
    def _():
        kaug_ref[:, :LANES] = k_ref[...]
        pos = lax.broadcasted_iota(jnp.int32, (seq, LANES), 0)
        lane = lax.broadcasted_iota(jnp.int32, (seq, LANES), 1)
        feat = jnp.where(lane < 3, pos >> 6, jnp.where(lane < 6, pos & 63, 0))
        kaug_ref[:, LANES:] = feat.astype(F32).astype(BF16)
        vaug_ref[:, :LANES] = v_ref[...]
        vaug_ref[:, LANES:] = jnp.ones((seq, LANES), BF16)
        stage_queries(0, q_ref[...])
        for r in range(ROW_BLOCKS):
            scores_rows(0, 0, 0, r * rb, True)
        m_ref[...] = jnp.full(m_ref.shape, -jnp.inf, F32)
        acc_ref[...] = jnp.zeros(acc_ref.shape, F32)
        lam = (jnp.exp(jnp.sum(lq1_ref[...] * lk1_ref[...], keepdims=True)) -
               jnp.exp(jnp.sum(lq2_ref[...] * lk2_ref[...], keepdims=True)) + lambda_init)
        lam_ref[...] = jnp.broadcast_to(lam, lam_ref.shape)

    qslot = qi % 2
    base = (qi * (qi + 1) // 2) % 2

    def two_steps(k, parity):
        def both(slot):
            for r in range(ROW_BLOCKS):
                accumulate_rows(k, slot, r * rb, False)
                scores_rows(qslot, k + 1, 1 - slot, r * rb, False)
            for r in range(ROW_BLOCKS):
                accumulate_rows(k + 1, 1 - slot, r * rb, False)
                scores_rows(qslot, k + 2, slot, r * rb, False)

        @pl.when(parity == 0)
        def _():
            both(0)

        @pl.when(parity == 1)
        def _():
            both(1)

    n_loop = jnp.maximum(qi - 1, 0)

    def body(j, carry):
        two_steps(2 * j, base)
        return carry

    lax.fori_loop(0, n_loop // 2, body, 0)

    @pl.when(n_loop % 2 == 1)
    def _():
        step(n_loop - 1, (base + n_loop - 1) % 2, qslot, n_loop, False)

    @pl.when(qi > 0)
    def _():
        step(qi - 1, (base + qi - 1) % 2, qslot, qi, True)

    last_tile = seq // t - 1

    @pl.when(qi < last_tile)
    def _():
        last_step(qi, (base + qi) % 2, 1 - qslot, True)

    @pl.when(qi == last_tile)
    def _():
        last_step(qi, (base + qi) % 2, 1 - qslot, False)


def _slope_features(slopes):
    feats = np.zeros((HEADS, 1, LANES), np.float32)
    for h, slope in enumerate(slopes):
        for base, c in ((0, 64.0 * LOG2E * slope), (3, LOG2E * slope)):
            rest = np.float64(c)
            for j in range(3):
                piece = np.float64(np.float32(rest).astype(BF16))
                feats[h, 0, base + j] = piece
                rest = rest - piece
    return jnp.asarray(feats, dtype=BF16)


def _diff_attn(qkv, qfeat, lq1, lk1, lq2, lk2, g, *, batch, seq, t, q_col, k_col, v_col,
               lambda_init):
    n = qkv.shape[1]
    nq = seq // t
    vec = lambda: pl.BlockSpec((1, HEAD_DK), lambda b, h, i: (0, 0))
    return pl.pallas_call(
        functools.partial(_attn_kernel, t=t, seq=seq, lambda_init=lambda_init),
        out_shape=jax.ShapeDtypeStruct((n, HEADS * HEAD_DV), BF16),
        grid=(batch, HEADS, nq),
        in_specs=[
            pl.BlockSpec((1, 1, LANES), lambda b, h, i: (h, 0, 0)),
            pl.BlockSpec((None, t, LANES), lambda b, h, i: (q_col + h, b * nq + i, 0)),
            pl.BlockSpec((None, t, LANES),
                         lambda b, h, i: (q_col + h, b * nq + jnp.minimum(i + 1, nq - 1), 0)),
            pl.BlockSpec((None, seq, LANES), lambda b, h, i: (k_col + h, b, 0)),
            pl.BlockSpec((None, seq, LANES), lambda b, h, i: (v_col + h, b, 0)),
            vec(), vec(), vec(), vec(),
            pl.BlockSpec((1, HEAD_DV), lambda b, h, i: (0, 0)),
        ],
        out_specs=pl.BlockSpec((t, HEAD_DV), lambda b, h, i: (b * nq + i, h)),
        scratch_shapes=[
            pltpu.VMEM((seq, 2 * LANES), BF16),
            pltpu.VMEM((seq, 2 * LANES), BF16),
            pltpu.VMEM((2, 2 * t, 2 * LANES), BF16),
            pltpu.VMEM((2, 2 * t, t), F32),
            pltpu.VMEM((2, 2 * t, LANES), F32),
            pltpu.VMEM((2 * t, LANES), F32),
            pltpu.VMEM((2 * t, 2 * HEAD_DV), F32),
            pltpu.VMEM((1, HEAD_DV), F32),
        ],
        compiler_params=pltpu.CompilerParams(
            dimension_semantics=("parallel", "parallel", "arbitrary"),
            vmem_limit_bytes=VMEM_LIMIT_BYTES),
        name="diff_attn",
    )(qfeat, qkv, qkv, qkv, qkv, lq1, lk1, lq2, lk2, g)


def _out_proj_kernel(ya_ref, yb_ref, wa_ref, wb_ref, x_ref, g_ref, o_ref):
    y = jnp.dot(ya_ref[...], wa_ref[...], preferred_element_type=F32)
    y = y + jnp.dot(yb_ref[...], wb_ref[...], preferred_element_type=F32)
    o_ref[...] = x_ref[...] + _rms(y, g_ref[...])


def _out_proj(ya, yb, w, x2, g, *, tm):
    n, d = x2.shape
    ka, kb = ya.shape[1], yb.shape[1]
    return pl.pallas_call(
        _out_proj_kernel,
        out_shape=jax.ShapeDtypeStruct((n, d), F32),
        grid=(n // tm,),
        in_specs=[
            pl.BlockSpec((tm, ka), lambda i: (i, 0)),
            pl.BlockSpec((tm, kb), lambda i: (i, 0)),
            pl.BlockSpec((ka, d), lambda i: (0, 0)),
            pl.BlockSpec((kb, d), lambda i: (1, 0)),
            pl.BlockSpec((tm, d), lambda i: (i, 0)),
            pl.BlockSpec((1, d), lambda i: (0, 0)),
        ],
        out_specs=pl.BlockSpec((tm, d), lambda i: (i, 0)),
        compiler_params=pltpu.CompilerParams(
            dimension_semantics=("parallel",),
            vmem_limit_bytes=VMEM_LIMIT_BYTES),
        name="out_proj",
    )(ya, yb, w, w, x2, g)


def _mlp_kernel(h_ref, g1_ref, wu_ref, wd_ref, g2_ref, o_ref, hn_ref):
    acc_ref = o_ref
    f = pl.program_id(1)
    last = pl.num_programs(1) - 1

    def down(hn):
        up = jnp.dot(hn, wu_ref[...], preferred_element_type=F32)
        act = jnp.square(jnp.maximum(up, 0.0)).astype(BF16)
        return jnp.dot(act, wd_ref[...], preferred_element_type=F32)

    @pl.when(f == 0)
    def _():
        hn = _rms(h_ref[...], g1_ref[...]).astype(BF16)
        hn_ref[...] = hn
        acc_ref[...] = down(hn)

    @pl.when(jnp.logical_and(f > 0, f < last))
    def _():
        acc_ref[...] += down(hn_ref[...])

    @pl.when(f == last)
    def _():
        y = acc_ref[...] + down(hn_ref[...])
        o_ref[...] = h_ref[...] + _rms(y, g2_ref[...])


def _mlp(h1, g1, wu, wd, g2, *, tm, tf):
    n, d = h1.shape
    n_tiles = wu.shape[1] // tf
    return pl.pallas_call(
        _mlp_kernel,
        out_shape=jax.ShapeDtypeStruct((n, d), F32),
        grid=(n // tm, n_tiles),
        in_specs=[
            pl.BlockSpec((tm, d), lambda i, f: (i, 0)),
            pl.BlockSpec((1, d), lambda i, f: (0, 0)),
            pl.BlockSpec((d, tf), lambda i, f: (0, f)),
            pl.BlockSpec((tf, d), lambda i, f: (f, 0)),
            pl.BlockSpec((1, d), lambda i, f: (0, 0)),
        ],
        out_specs=pl.BlockSpec((tm, d), lambda i, f: (i, 0)),
        scratch_shapes=[pltpu.VMEM((tm, d), BF16)],
        compiler_params=pltpu.CompilerParams(
            dimension_semantics=("parallel", "arbitrary"),
            vmem_limit_bytes=VMEM_LIMIT_BYTES),
        name="mlp",
    )(h1, g1, wu, wd, g2)


def kernel(x, pre_mix_g, w_in, gmlp_ln_g, gmlp_ln_b, gmlp_w_s, gmlp_b_s, lambda_q1, lambda_k1,
           lambda_q2, lambda_k2, diff_subln_g, w_out, post_mix_g, pre_mlp_g, w_up, w_down,
           post_mlp_g):
    batch, seq, d = x.shape
    n = batch * seq
    depth = w_in.shape[0]
    gmlp_cols = 2 * GROUPS * GROUP_CH
    qk_cols = HEADS * 2 * HEAD_DK
    assert w_in.shape[2] == gmlp_cols + 2 * qk_cols + HEADS * HEAD_DV
    assert seq % CHUNK == 0

    tm_in = min(1024, n)
    tn_in = 512
    t_attn = min(1024, seq)
    tm_out = min(512, n)
    tm_mlp = min(1024, n)
    tf_mlp = min(512, w_up.shape[2])

    qfeat = _slope_features(2.0 ** (-8.0 * np.arange(1, HEADS + 1) / HEADS))
    row = lambda a: a.reshape(1, -1).astype(F32)

    h = x.reshape(n, d)
    for l in range(depth):
        lambda_init = 0.8 - 0.6 * math.exp(-0.3 * l)
        y_a, qkv = _in_proj(h, row(pre_mix_g[l]), w_in[l].astype(BF16),
                            gmlp_ln_g[l].reshape(GROUPS, 1, GROUP_CH).astype(F32),
                            gmlp_ln_b[l].reshape(GROUPS, 1, GROUP_CH).astype(F32),
                            gmlp_w_s[l].astype(F32),
                            gmlp_b_s[l].reshape(GROUPS, CHUNK, 1).astype(F32),
                            tm=tm_in, tn=tn_in)
        y_b = _diff_attn(qkv, qfeat, row(lambda_q1[l]), row(lambda_k1[l]),
                         row(lambda_q2[l]), row(lambda_k2[l]), row(diff_subln_g[l]),
                         batch=batch, seq=seq, t=t_attn,
                         q_col=0, k_col=qk_cols // LANES, v_col=2 * qk_cols // LANES,
                         lambda_init=lambda_init)
        h1 = _out_proj(y_a, y_b, w_out[l].astype(BF16), h, row(post_mix_g[l]), tm=tm_out)
        h = _mlp(h1, row(pre_mlp_g[l]), w_up[l].astype(BF16), w_down[l].astype(BF16),
                 row(post_mlp_g[l]), tm=tm_mlp, tf=tf_mlp)
    return h.reshape(batch, seq, d)
```
